```python
import math
import jax, jax.numpy as jnp
from jax import lax
import numpy as np

D_MODEL = 1024
BATCH = 16
SEQ = 2048
DEPTH = 1

PLE_DIM = 256
WIDTH_A = 1024
CONV_A = 3
WIDTH_B = 1024
CONV_B = 31
N_GROUPS = 4
EXPERTS_PER_GROUP = 8
N_EXPERTS = N_GROUPS * EXPERTS_PER_GROUP
TOP_K_IN_GROUP = 2
D_EXPERT = 512
ROW_BLOCK = 128
EPS = 1e-6
IN_COLS = 3 * WIDTH_A + 2 * WIDTH_B + 2 * D_MODEL

kernel_name = "hybrid_conv_conformer_hiermoe_ple"


def rmsnorm(x, g):
    xf = x.astype(jnp.float32)
    y = xf * lax.rsqrt(jnp.mean(xf * xf, axis=-1, keepdims=True) + EPS)
    return (y * g.astype(jnp.float32)).astype(x.dtype)


def layernorm(x, g, b):
    xf = x.astype(jnp.float32)
    mu = jnp.mean(xf, axis=-1, keepdims=True)
    var = jnp.mean(jnp.square(xf - mu), axis=-1, keepdims=True)
    y = (xf - mu) * lax.rsqrt(var + EPS)
    return (y * g.astype(jnp.float32) + b.astype(jnp.float32)).astype(x.dtype)


def causal_dwconv(u, w):
    k, c = w.shape
    return lax.conv_general_dilated(
        u, w.astype(u.dtype)[:, None, :], window_strides=(1,), padding=[(k - 1, 0)],
        dimension_numbers=("NWC", "WIO", "NWC"), feature_group_count=c)


def token_mixer(h, w_in, conv_a_w, w_out_a, conv_b_w, conv_b_b, ln_b_g, ln_b_b, w_out_b, b_gate, w_o):
    proj = h @ w_in
    cuts = [WIDTH_A, 2 * WIDTH_A, 3 * WIDTH_A, 3 * WIDTH_A + WIDTH_B, 3 * WIDTH_A + 2 * WIDTH_B]
    a_b, a_c, a_x, c_val, c_gate, gates = jnp.split(proj, cuts, axis=-1)
    y_a = (a_b * causal_dwconv(a_c * a_x, conv_a_w)) @ w_out_a
    u = c_val * jax.nn.sigmoid(c_gate)
    u = causal_dwconv(u, conv_b_w) + conv_b_b
    u = jax.nn.silu(layernorm(u, ln_b_g, ln_b_b))
    y_b = u @ w_out_b
    g = jax.nn.sigmoid(gates + b_gate)
    g_a, g_b = jnp.split(g, 2, axis=-1)
    return (g_a * y_a + g_b * y_b) @ w_o


def hierarchical_moe(h, w_rg, b_rg, w_re, b_re, w_gate, w_up, w_down):
    bsz, seq, d = h.shape
    t = bsz * seq
    hf = h.reshape(t, d)
    g_logits = (hf @ w_rg + b_rg).astype(jnp.float32)
    g_prob = jax.nn.softmax(g_logits, axis=-1)
    grp = jnp.argmax(g_logits, axis=-1).astype(jnp.int32)
    p_grp = jnp.take_along_axis(g_prob, grp[:, None], axis=-1)
    e_logits = (hf @ w_re + b_re).astype(jnp.float32).reshape(t, N_GROUPS, EXPERTS_PER_GROUP)
    e_logits = jnp.take_along_axis(e_logits, grp[:, None, None], axis=1)[:, 0]
    top_v, top_i = lax.top_k(e_logits, TOP_K_IN_GROUP)
    gate_w = jax.nn.softmax(top_v, axis=-1) * p_grp
    expert_id = grp[:, None] * EXPERTS_PER_GROUP + top_i.astype(jnp.int32)

    n_assign = t * TOP_K_IN_GROUP
    flat_e = expert_id.reshape(n_assign)
    flat_w = gate_w.reshape(n_assign)
    order = jnp.argsort(flat_e)
    sorted_e = flat_e[order]
    tok = order // TOP_K_IN_GROUP
    counts = jnp.bincount(flat_e, length=N_EXPERTS)
    padded = (counts + ROW_BLOCK - 1) // ROW_BLOCK * ROW_BLOCK
    start = jnp.cumsum(counts) - counts
    pad_end = jnp.cumsum(padded)
    pad_start = pad_end - padded
    dest = pad_start[sorted_e] + (jnp.arange(n_assign, dtype=jnp.int32) - start[sorted_e])
    n_blocks = -(-(n_assign + N_EXPERTS * (ROW_BLOCK - 1)) // ROW_BLOCK)
    buf = jnp.zeros((n_blocks * ROW_BLOCK, d), hf.dtype).at[dest].set(hf[tok])
    block_start = jnp.arange(n_blocks, dtype=jnp.int32) * ROW_BLOCK
    block_expert = jnp.minimum(jnp.searchsorted(pad_end, block_start, side="right"), N_EXPERTS - 1)

    def expert_block(args):
        xb, e = args
        a = xb @ w_gate[e]
        b = xb @ w_up[e]
        return (jax.nn.silu(a) * b) @ w_down[e]

    y_buf = lax.map(expert_block, (buf.reshape(n_blocks, ROW_BLOCK, d), block_expert)).reshape(-1, d)
    y = y_buf[dest] * flat_w[order][:, None].astype(hf.dtype)
    out = jax.ops.segment_sum(y, tok, num_segments=t)
    return out.reshape(bsz, seq, d)


def setup_inputs(seed: int = 0) -> dict:
    key = jax.random.key(seed)
    ks = jax.random.split(key, 32)
    n = lambda k, shape, s: jax.random.normal(k, shape, jnp.float32) * s
    L, D = DEPTH, D_MODEL
    return {
        "x": n(ks[0], (BATCH, SEQ, D), 1.0),
        "p": n(ks[1], (DEPTH, BATCH, SEQ, PLE_DIM), 1.0),
        "mix_norm_g": 1.0 + n(ks[2], (L, D), 0.02),
        "w_in": n(ks[3], (L, D, IN_COLS), D ** -0.5),
        "conv_a_w": n(ks[4], (L, CONV_A, WIDTH_A), CONV_A ** -0.5),
        "w_out_a": n(ks[5], (L, WIDTH_A, D), WIDTH_A ** -0.5),
        "conv_b_w": n(ks[6], (L, CONV_B, WIDTH_B), CONV_B ** -0.5),
        "conv_b_b": n(ks[7], (L, WIDTH_B), 0.02),
        "ln_b_g": 1.0 + n(ks[8], (L, WIDTH_B), 0.02),
        "ln_b_b": n(ks[9], (L, WIDTH_B), 0.02),
        "w_out_b": n(ks[10], (L, WIDTH_B, D), WIDTH_B ** -0.5),
        "b_gate": n(ks[11], (L, 2 * D), 0.02),
        "w_o": n(ks[12], (L, D, D), D ** -0.5),
        "ffn_norm_g": 1.0 + n(ks[13], (L, D), 0.02),
        "w_router_group": n(ks[14], (L, D, N_GROUPS), D ** -0.5),
        "b_router_group": n(ks[15], (L, N_GROUPS), 0.01),
        "w_router_expert": n(ks[16], (L, D, N_EXPERTS), D ** -0.5),
        "b_router_expert": n(ks[17], (L, N_EXPERTS), 0.01),
        "w_exp_gate": n(ks[18], (L, N_EXPERTS, D, D_EXPERT), D ** -0.5),
        "w_exp_up": n(ks[19], (L, N_EXPERTS, D, D_EXPERT), D ** -0.5),
        "w_exp_down": n(ks[20], (L, N_EXPERTS, D_EXPERT, D), D_EXPERT ** -0.5),
        "ple_norm_g": 1.0 + n(ks[21], (L, D), 0.02),
        "w_ple_gate": n(ks[22], (L, D, D), D ** -0.5),
        "w_ple_proj": n(ks[23], (L, PLE_DIM, D), PLE_DIM ** -0.5),
        "final_norm_g": 1.0 + n(ks[24], (D,), 0.02),
    }


def reference(x, p, mix_norm_g, w_in, conv_a_w, w_out_a, conv_b_w, conv_b_b, ln_b_g, ln_b_b,
              w_out_b, b_gate, w_o, ffn_norm_g, w_router_group, b_router_group, w_router_expert,
              b_router_expert, w_exp_gate, w_exp_up, w_exp_down, ple_norm_g, w_ple_gate, w_ple_proj,
              final_norm_g):
    for i in range(DEPTH):
        h = rmsnorm(x, mix_norm_g[i])
        x = x + token_mixer(h, w_in[i], conv_a_w[i], w_out_a[i], conv_b_w[i], conv_b_b[i],
                            ln_b_g[i], ln_b_b[i], w_out_b[i], b_gate[i], w_o[i])
        h = rmsnorm(x, ffn_norm_g[i])
        x = x + hierarchical_moe(h, w_router_group[i], b_router_group[i], w_router_expert[i],
                                 b_router_expert[i], w_exp_gate[i], w_exp_up[i], w_exp_down[i])
        hp = rmsnorm(x, ple_norm_g[i])
        x = x + jax.nn.sigmoid(hp @ w_ple_gate[i]) * (p[i] @ w_ple_proj[i])
    return rmsnorm(x, final_norm_g)
```

```python
import functools

import jax
import jax.numpy as jnp
from jax import lax
from jax.experimental import pallas as pl
from jax.experimental.pallas import tpu as pltpu

F32 = jnp.float32
BF16 = jnp.bfloat16
EPS = 1e-6

LANES = 128
N_GROUPS = 4
EXPERTS_PER_GROUP = 8
N_EXPERTS = N_GROUPS * EXPERTS_PER_GROUP
TOP_K = 2
ROUTER_ROWS = 8 + N_EXPERTS

SEQ_TILE = 512
CONV_ROWS = 64
HALO_A = 8
HALO_B = 32
SCATTER_TILE = 512
EXPERT_TILE = 256
COMBINE_TILE = 256
VMEM_LIMIT = 56 * 1024 * 1024


def _sigmoid(v):
    return 1.0 / (1.0 + jnp.exp(-v))


def _rms(v, g):
    return v * lax.rsqrt(jnp.mean(v * v, axis=-1, keepdims=True) + EPS) * g


def _dot(a, b):
    return jnp.dot(a, b, preferred_element_type=F32)


def _causal_conv(buf, w_ref, out_ref, width, halo, rows):
    n_chunks = buf.shape[0]
    for j in range(n_chunks):
        lanes = slice(j * LANES, (j + 1) * LANES)

        def body(r, carry, j=j, lanes=lanes):
            base = pl.multiple_of(r * CONV_ROWS, CONV_ROWS)
            acc = jnp.zeros((CONV_ROWS, LANES), F32)
            for k in range(width):
                wk = w_ref[k:k + 1, lanes]
                acc = acc + wk * buf[j, pl.ds(base + (halo - (width - 1) + k), CONV_ROWS), :]
            out_ref[pl.ds(base, CONV_ROWS), lanes] = acc
            return carry

        lax.fori_loop(0, rows // CONV_ROWS, body, 0)


def _mixer_kernel(x_ref, g1_ref, win_ref, caw_ref, woa_ref, cbw_ref, cbb_ref, lng_ref, lnb_ref,
                  wob_ref, bg_ref, wo_ref, g2_ref, wr_ref, br_ref,
                  x1_ref, eid_ref, gw_ref, rank_ref, cnt_ref,
                  abuf, bbuf, cbuf, *, width_a, width_b, d_model, wa, wb):
    ts = x_ref.shape[0]
    first_tile = pl.program_id(1) == 0

    @pl.when(first_tile)
    def _():
        abuf[:, 0:HALO_A, :] = jnp.zeros((abuf.shape[0], HALO_A, LANES), F32)
        bbuf[:, 0:HALO_B, :] = jnp.zeros((bbuf.shape[0], HALO_B, LANES), F32)

    @pl.when(first_tile & (pl.program_id(0) == 0))
    def _():
        cnt_ref[...] = jnp.zeros(cnt_ref.shape, F32)

    x = x_ref[...]
    h = _rms(x, g1_ref[...]).astype(BF16)

    v = _dot(h, win_ref[:, wa:2 * wa]) * _dot(h, win_ref[:, 2 * wa:3 * wa])
    for j in range(wa // LANES):
        abuf[j, HALO_A:HALO_A + ts, :] = v[:, j * LANES:(j + 1) * LANES]
    _causal_conv(abuf, caw_ref, cbuf, width_a, HALO_A, ts)
    abuf[:, 0:HALO_A, :] = abuf[:, ts:ts + HALO_A, :]
    z = (_dot(h, win_ref[:, 0:wa]) * cbuf[...]).astype(BF16)
    gates = _sigmoid(_dot(h, win_ref[:, 3 * wa + 2 * wb:3 * wa + 2 * wb + d_model])
                     + bg_ref[:, 0:d_model])
    mix = gates * _dot(z, woa_ref[...])

    u = _dot(h, win_ref[:, 3 * wa:3 * wa + wb]) * _sigmoid(
        _dot(h, win_ref[:, 3 * wa + wb:3 * wa + 2 * wb]))
    for j in range(wb // LANES):
        bbuf[j, HALO_B:HALO_B + ts, :] = u[:, j * LANES:(j + 1) * LANES]
    _causal_conv(bbuf, cbw_ref, cbuf, width_b, HALO_B, ts)
    bbuf[:, 0:HALO_B, :] = bbuf[:, ts:ts + HALO_B, :]
    c = cbuf[...] + cbb_ref[...]
    mu = jnp.mean(c, axis=-1, keepdims=True)
    cc = c - mu
    var = jnp.mean(cc * cc, axis=-1, keepdims=True)
    ln = cc * lax.rsqrt(var + EPS) * lng_ref[...] + lnb_ref[...]
    sw = (ln * _sigmoid(ln)).astype(BF16)
    gates = _sigmoid(_dot(h, win_ref[:, 3 * wa + 2 * wb + d_model:3 * wa + 2 * wb + 2 * d_model])
                     + bg_ref[:, d_model:2 * d_model])
    mix = mix + gates * _dot(sw, wob_ref[...])

    x1 = x + _dot(mix.astype(BF16), wo_ref[...])
    x1_ref[...] = x1

    h2 = _rms(x1, g2_ref[...])
    logits = lax.dot_general(wr_ref[...], h2, (((1,), (1,)), ((), ())),
                             preferred_element_type=F32) + br_ref[:, 0:1]
    gl = logits[0:N_GROUPS, :]
    gmax = jnp.max(gl, axis=0, keepdims=True)
    giota = lax.broadcasted_iota(jnp.int32, gl.shape, 0)
    grp = jnp.min(jnp.where(gl == gmax, giota, N_GROUPS), axis=0, keepdims=True)
    p_grp = 1.0 / jnp.sum(jnp.exp(gl - gmax), axis=0, keepdims=True)

    el = logits[8:8 + N_EXPERTS, :]
    eiota = lax.broadcasted_iota(jnp.int32, el.shape, 0)
    neg = jnp.float32(-jnp.inf)
    m0 = jnp.where(lax.div(eiota, EXPERTS_PER_GROUP) == grp, el, neg)
    v1 = jnp.max(m0, axis=0, keepdims=True)
    i1 = jnp.min(jnp.where(m0 == v1, eiota, N_EXPERTS), axis=0, keepdims=True)
    m1 = jnp.where(eiota == i1, neg, m0)
    v2 = jnp.max(m1, axis=0, keepdims=True)
    i2 = jnp.min(jnp.where(m1 == v2, eiota, N_EXPERTS), axis=0, keepdims=True)
    e2 = jnp.exp(v2 - v1)
    w1 = p_grp / (1.0 + e2)
    w2 = p_grp * e2 / (1.0 + e2)

    oh1 = eiota == i1
    oh2 = eiota == i2
    used = jnp.where(oh1 | oh2, 1.0, 0.0)
    rio = lax.broadcasted_iota(jnp.int32, (ts, ts), 0)
    cio = lax.broadcasted_iota(jnp.int32, (ts, ts), 1)
    before = jnp.where(rio < cio, 1.0, 0.0).astype(BF16)
    prior = _dot(used.astype(BF16), before) + cnt_ref[:, 0:1]
    r1 = jnp.sum(jnp.where(oh1, prior, 0.0), axis=0, keepdims=True)
    r2 = jnp.sum(jnp.where(oh2, prior, 0.0), axis=0, keepdims=True)
    cnt_ref[...] = cnt_ref[...] + jnp.sum(used, axis=1, keepdims=True)

    eid_ref[...] = jnp.concatenate([i1, i2], axis=0)
    gw_ref[...] = jnp.concatenate([w1, w2], axis=0)
    rank_ref[...] = jnp.concatenate([r1, r2], axis=0).astype(jnp.int32)


def _const_spec(shape):
    return pl.BlockSpec(shape, lambda *_: (0,) * len(shape), pipeline_mode=pl.Buffered(1))


def _mixer(x, g1, w_in, conv_a_w, w_out_a, conv_b_w, conv_b_b, ln_g, ln_b, w_out_b, b_gate, w_o,
           g2, w_r, b_r):
    bsz, seq, d = x.shape
    wa = w_out_a.shape[0]
    wb = w_out_b.shape[0]
    ts = SEQ_TILE
    n_s = seq // ts
    t = bsz * seq
    consts = (g1, w_in, conv_a_w, w_out_a, conv_b_w, conv_b_b, ln_g, ln_b, w_out_b, b_gate, w_o,
              g2, w_r, b_r)
    tok_spec = pl.BlockSpec((TOP_K, ts), lambda b, s: (0, b * n_s + s))
    return pl.pallas_call(
        functools.partial(_mixer_kernel, width_a=conv_a_w.shape[0], width_b=conv_b_w.shape[0],
                          d_model=d, wa=wa, wb=wb),
        grid=(bsz, n_s),
        in_specs=[pl.BlockSpec((None, ts, d), lambda b, s: (b, s, 0))]
                 + [_const_spec(c.shape) for c in consts],
        out_specs=[pl.BlockSpec((None, ts, d), lambda b, s: (b, s, 0)),
                   tok_spec, tok_spec, tok_spec,
                   pl.BlockSpec((N_EXPERTS, LANES), lambda b, s: (0, 0))],
        out_shape=[jax.ShapeDtypeStruct((bsz, seq, d), F32),
                   jax.ShapeDtypeStruct((TOP_K, t), jnp.int32),
                   jax.ShapeDtypeStruct((TOP_K, t), F32),
                   jax.ShapeDtypeStruct((TOP_K, t), jnp.int32),
                   jax.ShapeDtypeStruct((N_EXPERTS, LANES), F32)],
        scratch_shapes=[pltpu.VMEM((wa // LANES, HALO_A + ts, LANES), F32),
                        pltpu.VMEM((wb // LANES, HALO_B + ts, LANES), F32),
                        pltpu.VMEM((ts, max(wa, wb)), F32)],
        compiler_params=pltpu.CompilerParams(
            dimension_semantics=("arbitrary", "arbitrary"), vmem_limit_bytes=VMEM_LIMIT),
        name="mixer",
    )(x, *consts)


def _row_copy(src, i, dst, j, sem):
    return pltpu.make_async_copy(src.at[pl.ds(i, 1), :], dst.at[pl.ds(j, 1), :], sem)


def _scatter_kernel(dest_ref, x1_ref, g2_ref, buf_ref, hbuf, sem):
    tt = x1_ref.shape[0]
    hbuf[...] = _rms(x1_ref[...], g2_ref[...])

    def issue(t, carry):
        for k in range(TOP_K):
            _row_copy(hbuf, t, buf_ref, dest_ref[k, t], sem).start()
        return carry

    def drain(t, carry):
        for k in range(TOP_K):
            _row_copy(hbuf, t, buf_ref, dest_ref[k, t], sem).wait()
        return carry

    lax.fori_loop(0, tt, issue, 0)
    lax.fori_loop(0, tt, drain, 0)


def _scatter(x1, g2, dest):
    t, d = x1.shape
    tt = SCATTER_TILE
    return pl.pallas_call(
        _scatter_kernel,
        grid=(t // tt,),
        in_specs=[pl.BlockSpec((None, TOP_K, tt), lambda i: (i, 0, 0), memory_space=pltpu.SMEM),
                  pl.BlockSpec((tt, d), lambda i: (i, 0)),
                  pl.BlockSpec((1, d), lambda i: (0, 0))],
        out_specs=pl.BlockSpec(memory_space=pl.ANY),
        out_shape=jax.ShapeDtypeStruct((TOP_K * t, d), F32),
        scratch_shapes=[pltpu.VMEM((tt, d), F32), pltpu.SemaphoreType.DMA(())],
        compiler_params=pltpu.CompilerParams(dimension_semantics=("arbitrary",)),
        name="scatter",
    )(dest.reshape(TOP_K, t // tt, tt).transpose(1, 0, 2), x1, g2)


def _expert_kernel(gid_ref, tid_ref, nv_ref, offs_ref, xb_ref, wg_ref, wu_ref, wd_ref, y_ref):
    i = pl.program_id(0)
    rb = xb_ref.shape[0]

    @pl.when(i < nv_ref[0])
    def _():
        g = gid_ref[i]
        tile = tid_ref[i]
        xb = xb_ref[...].astype(BF16)
        a = _dot(xb, wg_ref[...])
        hid = (a * _sigmoid(a) * _dot(xb, wu_ref[...])).astype(BF16)
        y = _dot(hid, wd_ref[...])
        prev_tile = tid_ref[jnp.maximum(i - 1, 0)]
        first_visit = (i == 0) | (prev_tile != tile)

        @pl.when(first_visit)
        def _():
            y_ref[...] = y

        @pl.when(jnp.logical_not(first_visit))
        def _():
            rows = tile * rb + lax.broadcasted_iota(jnp.int32, y.shape, 0)
            mine = (rows >= offs_ref[g]) & (rows < offs_ref[g + 1])
            y_ref[...] = jnp.where(mine, y, y_ref[...])


def _experts(buf, w_gate, w_up, w_down, gid, tid, nv, offs):
    n, d = buf.shape
    n_exp, _, de = w_gate.shape
    rb = EXPERT_TILE
    n_visits = gid.shape[0]
    grid_spec = pltpu.PrefetchScalarGridSpec(
        num_scalar_prefetch=4,
        grid=(n_visits,),
        in_specs=[pl.BlockSpec((rb, d), lambda i, gid, tid, nv, offs: (tid[i], 0)),
                  pl.BlockSpec((None, d, de), lambda i, gid, tid, nv, offs: (gid[i], 0, 0)),
                  pl.BlockSpec((None, d, de), lambda i, gid, tid, nv, offs: (gid[i], 0, 0)),
                  pl.BlockSpec((None, de, d), lambda i, gid, tid, nv, offs: (gid[i], 0, 0))],
        out_specs=pl.BlockSpec((rb, d), lambda i, gid, tid, nv, offs: (tid[i], 0)),
    )
    return pl.pallas_call(
        _expert_kernel,
        grid_spec=grid_spec,
        out_shape=jax.ShapeDtypeStruct((n, d), F32),
        compiler_params=pltpu.CompilerParams(dimension_semantics=("arbitrary",),
                                             vmem_limit_bytes=VMEM_LIMIT),
        name="experts",
    )(gid, tid, nv, offs, buf, w_gate, w_up, w_down)


def _combine_kernel(dest_ref, x1_ref, p_ref, gw_ref, g3_ref, wpg_ref, wpp_ref, gf_ref, ybuf_ref,
                    out_ref, rows, sem):
    tt = x1_ref.shape[0]

    def issue(t, carry):
        for k in range(TOP_K):
            _row_copy(ybuf_ref, dest_ref[k, t], rows.at[k], t, sem).start()
        return carry

    def drain(t, carry):
        for k in range(TOP_K):
            _row_copy(ybuf_ref, dest_ref[k, t], rows.at[k], t, sem).wait()
        return carry

    lax.fori_loop(0, tt, issue, 0)
    proj = _dot(p_ref[...].astype(BF16), wpp_ref[...])
    lax.fori_loop(0, tt, drain, 0)

    x2 = x1_ref[...] + gw_ref[:, 0:1] * rows[0] + gw_ref[:, 1:2] * rows[1]
    hp = _rms(x2, g3_ref[...]).astype(BF16)
    x3 = x2 + _sigmoid(_dot(hp, wpg_ref[...])) * proj
    out_ref[...] = _rms(x3, gf_ref[...])


def _combine(x1, p, gw_t, dest, ybuf, g3, w_pg, w_pp, gf):
    t, d = x1.shape
    pd = p.shape[1]
    tt = COMBINE_TILE
    return pl.pallas_call(
        _combine_kernel,
        grid=(t // tt,),
        in_specs=[pl.BlockSpec((None, TOP_K, tt), lambda i: (i, 0, 0), memory_space=pltpu.SMEM),
                  pl.BlockSpec((tt, d), lambda i: (i, 0)),
                  pl.BlockSpec((tt, pd), lambda i: (i, 0)),
                  pl.BlockSpec((tt, TOP_K), lambda i: (i, 0)),
                  pl.BlockSpec((1, d), lambda i: (0, 0)),
                  pl.BlockSpec((d, d), lambda i: (0, 0)),
                  pl.BlockSpec((pd, d), lambda i: (0, 0)),
                  pl.BlockSpec((1, d), lambda i: (0, 0)),
                  pl.BlockSpec(memory_space=pl.ANY)],
        out_specs=pl.BlockSpec((tt, d), lambda i: (i, 0)),
        out_shape=jax.ShapeDtypeStruct((t, d), F32),
        scratch_shapes=[pltpu.VMEM((TOP_K, tt, d), F32), pltpu.SemaphoreType.DMA(())],
        compiler_params=pltpu.CompilerParams(dimension_semantics=("arbitrary",)),
        name="combine",
    )(dest.reshape(TOP_K, t // tt, tt).transpose(1, 0, 2), x1, p, gw_t, g3, w_pg, w_pp, gf, ybuf)


def _visit_plan(cnt, n_rows):
    rb = EXPERT_TILE
    offs = jnp.concatenate([jnp.zeros((1,), jnp.int32), jnp.cumsum(cnt, dtype=jnp.int32)])
    first = offs[:-1] // rb
    n_tiles = jnp.where(cnt > 0, (offs[1:] - 1) // rb - first + 1, 0)
    ends = jnp.cumsum(n_tiles, dtype=jnp.int32)
    nv = ends[-1]
    max_visits = n_rows // rb + N_EXPERTS - 1
    vi = jnp.minimum(jnp.arange(max_visits, dtype=jnp.int32), nv - 1)
    gid = jnp.minimum(jnp.searchsorted(ends, vi, side="right").astype(jnp.int32), N_EXPERTS - 1)
    tid = first[gid] + vi - (ends[gid] - n_tiles[gid])
    return gid, tid, nv.reshape(1), offs


def _layer(x, p, mix_norm_g, w_in, conv_a_w, w_out_a, conv_b_w, conv_b_b, ln_b_g, ln_b_b, w_out_b,
           b_gate, w_o, ffn_norm_g, w_rg, b_rg, w_re, b_re, w_eg, w_eu, w_ed, ple_norm_g, w_pg, w_pp,
           out_norm_g):
    bsz, seq, d = x.shape
    t = bsz * seq
    row = lambda v: v.reshape(1, -1).astype(F32)
    w_r = jnp.zeros((ROUTER_ROWS, d), F32).at[0:N_GROUPS].set(w_rg.T).at[8:].set(w_re.T)
    b_r = jnp.zeros((ROUTER_ROWS,), F32).at[0:N_GROUPS].set(b_rg).at[8:].set(b_re)
    b_r = jnp.broadcast_to(b_r[:, None], (ROUTER_ROWS, LANES))

    x1, eid, gw, rank, cnt = _mixer(
        x, row(mix_norm_g), w_in.astype(BF16), conv_a_w, w_out_a.astype(BF16), conv_b_w,
        row(conv_b_b), row(ln_b_g), row(ln_b_b), w_out_b.astype(BF16), row(b_gate),
        w_o.astype(BF16), row(ffn_norm_g), w_r, b_r)
    x1 = x1.reshape(t, d)

    cnt = cnt[:, 0].astype(jnp.int32)
    gid, tid, nv, offs = _visit_plan(cnt, TOP_K * t)
    dest = offs[eid] + rank

    buf = _scatter(x1, row(ffn_norm_g), dest)
    ybuf = _experts(buf, w_eg.astype(BF16), w_eu.astype(BF16), w_ed.astype(BF16), gid, tid, nv, offs)
    out = _combine(x1, p.reshape(t, -1), gw.T, dest, ybuf, row(ple_norm_g), w_pg.astype(BF16),
                   w_pp.astype(BF16), row(out_norm_g))
    return out.reshape(bsz, seq, d)


def kernel(x, p, mix_norm_g, w_in, conv_a_w, w_out_a, conv_b_w, conv_b_b, ln_b_g, ln_b_b, w_out_b, b_gate, w_o, ffn_norm_g, w_router_group, b_router_group, w_router_expert, b_router_expert, w_exp_gate, w_exp_up, w_exp_down, ple_norm_g, w_ple_gate, w_ple_proj, final_norm_g):
    assert w_in.shape[0] == 1, "the combine kernel fuses the final RMSNorm: single layer only"
    return _layer(x, p[0], mix_norm_g[0], w_in[0], conv_a_w[0], w_out_a[0], conv_b_w[0],
                  conv_b_b[0], ln_b_g[0], ln_b_b[0], w_out_b[0], b_gate[0], w_o[0], ffn_norm_g[0],
                  w_router_group[0], b_router_group[0], w_router_expert[0], b_router_expert[0],
                  w_exp_gate[0], w_exp_up[0], w_exp_down[0], ple_norm_g[0], w_ple_gate[0],
                  w_ple_proj[0], final_norm_g)
```

```python
import functools

import jax
import jax.numpy as jnp
from jax import lax
from jax.experimental import pallas as pl
from jax.experimental.pallas import tpu as pltpu

F32 = jnp.float32
BF16 = jnp.bfloat16
EPS = 1e-6

LANES = 128
N_GROUPS = 4
EXPERTS_PER_GROUP = 8
N_EXPERTS = N_GROUPS * EXPERTS_PER_GROUP
TOP_K = 2
ROUTER_ROWS = 8 + N_EXPERTS

SEQ_TILE = 512
CONV_ROWS = 64
HALO_A = 8
HALO_B = 32
SCATTER_TILE = 512
EXPERT_TILE = 256
COMBINE_TILE = 256
DMA_UNROLL = 8
VMEM_LIMIT = 56 * 1024 * 1024


def _sigmoid(v):
    return 1.0 / (1.0 + jnp.exp(-v))


def _rms(v, g):
    return v * lax.rsqrt(jnp.mean(v * v, axis=-1, keepdims=True) + EPS) * g


def _dot(a, b):
    return jnp.dot(a, b, preferred_element_type=F32)


def _causal_conv(buf, w_ref, out_ref, width, halo, rows):
    n_chunks = buf.shape[0]
    for j in range(n_chunks):
        lanes = slice(j * LANES, (j + 1) * LANES)

        def body(r, carry, j=j, lanes=lanes):
            base = pl.multiple_of(r * CONV_ROWS, CONV_ROWS)
            acc = jnp.zeros((CONV_ROWS, LANES), F32)
            for k in range(width):
                wk = w_ref[k:k + 1, lanes]
                acc = acc + wk * buf[j, pl.ds(base + (halo - (width - 1) + k), CONV_ROWS), :]
            out_ref[pl.ds(base, CONV_ROWS), lanes] = acc
            return carry

        lax.fori_loop(0, rows // CONV_ROWS, body, 0)


def _mixer_kernel(x_ref, g1_ref, win_ref, caw_ref, woa_ref, cbw_ref, cbb_ref, lng_ref, lnb_ref,
                  wob_ref, bg_ref, wo_ref, g2_ref, wr_ref, br_ref,
                  x1_ref, eid_ref, gw_ref, rank_ref, cnt_ref,
                  abuf, bbuf, cbuf, *, width_a, width_b, d_model, wa, wb):
    ts = x_ref.shape[0]
    first_tile = pl.program_id(1) == 0

    @pl.when(first_tile)
    def _():
        abuf[:, 0:HALO_A, :] = jnp.zeros((abuf.shape[0], HALO_A, LANES), F32)
        bbuf[:, 0:HALO_B, :] = jnp.zeros((bbuf.shape[0], HALO_B, LANES), F32)

    @pl.when(first_tile & (pl.program_id(0) == 0))
    def _():
        cnt_ref[...] = jnp.zeros(cnt_ref.shape, F32)

    x = x_ref[...]
    h = _rms(x, g1_ref[...]).astype(BF16)

    v = _dot(h, win_ref[:, wa:2 * wa]) * _dot(h, win_ref[:, 2 * wa:3 * wa])
    for j in range(wa // LANES):
        abuf[j, HALO_A:HALO_A + ts, :] = v[:, j * LANES:(j + 1) * LANES]
    _causal_conv(abuf, caw_ref, cbuf, width_a, HALO_A, ts)
    abuf[:, 0:HALO_A, :] = abuf[:, ts:ts + HALO_A, :]
    z = (_dot(h, win_ref[:, 0:wa]) * cbuf[...]).astype(BF16)
    gates = _sigmoid(_dot(h, win_ref[:, 3 * wa + 2 * wb:3 * wa + 2 * wb + d_model])
                     + bg_ref[:, 0:d_model])
    mix = gates * _dot(z, woa_ref[...])

    u = _dot(h, win_ref[:, 3 * wa:3 * wa + wb]) * _sigmoid(
        _dot(h, win_ref[:, 3 * wa + wb:3 * wa + 2 * wb]))
    for j in range(wb // LANES):
        bbuf[j, HALO_B:HALO_B + ts, :] = u[:, j * LANES:(j + 1) * LANES]
    _causal_conv(bbuf, cbw_ref, cbuf, width_b, HALO_B, ts)
    bbuf[:, 0:HALO_B, :] = bbuf[:, ts:ts + HALO_B, :]
    c = cbuf[...] + cbb_ref[...]
    mu = jnp.mean(c, axis=-1, keepdims=True)
    cc = c - mu
    var = jnp.mean(cc * cc, axis=-1, keepdims=True)
    ln = cc * lax.rsqrt(var + EPS) * lng_ref[...] + lnb_ref[...]
    sw = (ln * _sigmoid(ln)).astype(BF16)
    gates = _sigmoid(_dot(h, win_ref[:, 3 * wa + 2 * wb + d_model:3 * wa + 2 * wb + 2 * d_model])
                     + bg_ref[:, d_model:2 * d_model])
    mix = mix + gates * _dot(sw, wob_ref[...])

    x1 = x + _dot(mix.astype(BF16), wo_ref[...])
    x1_ref[...] = x1

    h2 = _rms(x1, g2_ref[...])
    logits = lax.dot_general(wr_ref[...], h2, (((1,), (1,)), ((), ())),
                             preferred_element_type=F32) + br_ref[:, 0:1]
    gl = logits[0:N_GROUPS, :]
    gmax = jnp.max(gl, axis=0, keepdims=True)
    giota = lax.broadcasted_iota(jnp.int32, gl.shape, 0)
    grp = jnp.min(jnp.where(gl == gmax, giota, N_GROUPS), axis=0, keepdims=True)
    p_grp = 1.0 / jnp.sum(jnp.exp(gl - gmax), axis=0, keepdims=True)

    el = logits[8:8 + N_EXPERTS, :]
    eiota = lax.broadcasted_iota(jnp.int32, el.shape, 0)
    neg = jnp.float32(-jnp.inf)
    m0 = jnp.where(lax.div(eiota, EXPERTS_PER_GROUP) == grp, el, neg)
    v1 = jnp.max(m0, axis=0, keepdims=True)
    i1 = jnp.min(jnp.where(m0 == v1, eiota, N_EXPERTS), axis=0, keepdims=True)
    m1 = jnp.where(eiota == i1, neg, m0)
    v2 = jnp.max(m1, axis=0, keepdims=True)
    i2 = jnp.min(jnp.where(m1 == v2, eiota, N_EXPERTS), axis=0, keepdims=True)
    e2 = jnp.exp(v2 - v1)
    w1 = p_grp / (1.0 + e2)
    w2 = p_grp * e2 / (1.0 + e2)

    oh1 = eiota == i1
    oh2 = eiota == i2
    used = jnp.where(oh1 | oh2, 1.0, 0.0)
    rio = lax.broadcasted_iota(jnp.int32, (ts, ts), 0)
    cio = lax.broadcasted_iota(jnp.int32, (ts, ts), 1)
    before = jnp.where(rio < cio, 1.0, 0.0).astype(BF16)
    prior = _dot(used.astype(BF16), before) + cnt_ref[:, 0:1]
    r1 = jnp.sum(jnp.where(oh1, prior, 0.0), axis=0, keepdims=True)
    r2 = jnp.sum(jnp.where(oh2, prior, 0.0), axis=0, keepdims=True)
    cnt_ref[...] = cnt_ref[...] + jnp.sum(used, axis=1, keepdims=True)

    eid_ref[...] = jnp.concatenate([i1, i2], axis=0)
    gw_ref[...] = jnp.concatenate([w1, w2], axis=0)
    rank_ref[...] = jnp.concatenate([r1, r2], axis=0).astype(jnp.int32)


def _const_spec(shape):
    return pl.BlockSpec(shape, lambda *_: (0,) * len(shape), pipeline_mode=pl.Buffered(1))


def _mixer(x, g1, w_in, conv_a_w, w_out_a, conv_b_w, conv_b_b, ln_g, ln_b, w_out_b, b_gate, w_o,
           g2, w_r, b_r):
    bsz, seq, d = x.shape
    wa = w_out_a.shape[0]
    wb = w_out_b.shape[0]
    ts = SEQ_TILE
    n_s = seq // ts
    t = bsz * seq
    consts = (g1, w_in, conv_a_w, w_out_a, conv_b_w, conv_b_b, ln_g, ln_b, w_out_b, b_gate, w_o,
              g2, w_r, b_r)
    tok_spec = pl.BlockSpec((TOP_K, ts), lambda b, s: (0, b * n_s + s))
    return pl.pallas_call(
        functools.partial(_mixer_kernel, width_a=conv_a_w.shape[0], width_b=conv_b_w.shape[0],
                          d_model=d, wa=wa, wb=wb),
        grid=(bsz, n_s),
        in_specs=[pl.BlockSpec((None, ts, d), lambda b, s: (b, s, 0))]
                 + [_const_spec(c.shape) for c in consts],
        out_specs=[pl.BlockSpec((None, ts, d), lambda b, s: (b, s, 0)),
                   tok_spec, tok_spec, tok_spec,
                   pl.BlockSpec((N_EXPERTS, LANES), lambda b, s: (0, 0))],
        out_shape=[jax.ShapeDtypeStruct((bsz, seq, d), F32),
                   jax.ShapeDtypeStruct((TOP_K, t), jnp.int32),
                   jax.ShapeDtypeStruct((TOP_K, t), F32),
                   jax.ShapeDtypeStruct((TOP_K, t), jnp.int32),
                   jax.ShapeDtypeStruct((N_EXPERTS, LANES), F32)],
        scratch_shapes=[pltpu.VMEM((wa // LANES, HALO_A + ts, LANES), F32),
                        pltpu.VMEM((wb // LANES, HALO_B + ts, LANES), F32),
                        pltpu.VMEM((ts, max(wa, wb)), F32)],
        compiler_params=pltpu.CompilerParams(
            dimension_semantics=("arbitrary", "arbitrary"), vmem_limit_bytes=VMEM_LIMIT),
        name="mixer",
    )(x, *consts)


def _row_copy(src, i, dst, j, sem):
    return pltpu.make_async_copy(src.at[pl.ds(i, 1), :], dst.at[pl.ds(j, 1), :], sem)


def _scatter_kernel(dest_ref, x1_ref, g2_ref, buf_ref, hbuf, sem):
    tt = x1_ref.shape[0]
    hbuf[...] = _rms(x1_ref[...], g2_ref[...])

    def issue(c, carry):
        for u in range(DMA_UNROLL):
            t = c * DMA_UNROLL + u
            for k in range(TOP_K):
                _row_copy(hbuf, t, buf_ref, dest_ref[k, t], sem).start(priority=k)
        return carry

    lax.fori_loop(0, tt // DMA_UNROLL, issue, 0)
    for k in range(TOP_K):
        pltpu.make_async_copy(hbuf, buf_ref.at[pl.ds(0, tt), :], sem).wait()


def _scatter(x1, g2, dest):
    t, d = x1.shape
    tt = SCATTER_TILE
    return pl.pallas_call(
        _scatter_kernel,
        grid=(t // tt,),
        in_specs=[pl.BlockSpec((None, TOP_K, tt), lambda i: (i, 0, 0), memory_space=pltpu.SMEM),
                  pl.BlockSpec((tt, d), lambda i: (i, 0)),
                  pl.BlockSpec((1, d), lambda i: (0, 0))],
        out_specs=pl.BlockSpec(memory_space=pl.ANY),
        out_shape=jax.ShapeDtypeStruct((TOP_K * t, d), F32),
        scratch_shapes=[pltpu.VMEM((tt, d), F32), pltpu.SemaphoreType.DMA(())],
        compiler_params=pltpu.CompilerParams(dimension_semantics=("arbitrary",)),
        name="scatter",
    )(dest.reshape(TOP_K, t // tt, tt).transpose(1, 0, 2), x1, g2)


def _expert_kernel(gid_ref, tid_ref, nv_ref, offs_ref, xb_ref, wg_ref, wu_ref, wd_ref, y_ref,
                   wg_bf, wu_bf, wd_bf):
    i = pl.program_id(0)
    rb = xb_ref.shape[0]

    @pl.when(i < nv_ref[0])
    def _():
        g = gid_ref[i]
        tile = tid_ref[i]
        prev = jnp.maximum(i - 1, 0)

        @pl.when((i == 0) | (gid_ref[prev] != g))
        def _():
            wg_bf[...] = wg_ref[...].astype(BF16)
            wu_bf[...] = wu_ref[...].astype(BF16)
            wd_bf[...] = wd_ref[...].astype(BF16)

        xb = xb_ref[...].astype(BF16)
        a = _dot(xb, wg_bf[...])
        hid = (a * _sigmoid(a) * _dot(xb, wu_bf[...])).astype(BF16)
        y = _dot(hid, wd_bf[...])
        first_visit = (i == 0) | (tid_ref[prev] != tile)

        @pl.when(first_visit)
        def _():
            y_ref[...] = y

        @pl.when(jnp.logical_not(first_visit))
        def _():
            rows = tile * rb + lax.broadcasted_iota(jnp.int32, y.shape, 0)
            mine = (rows >= offs_ref[g]) & (rows < offs_ref[g + 1])
            y_ref[...] = jnp.where(mine, y, y_ref[...])


def _experts(buf, w_gate, w_up, w_down, gid, tid, nv, offs):
    n, d = buf.shape
    n_exp, _, de = w_gate.shape
    rb = EXPERT_TILE
    n_visits = gid.shape[0]
    grid_spec = pltpu.PrefetchScalarGridSpec(
        num_scalar_prefetch=4,
        grid=(n_visits,),
        in_specs=[pl.BlockSpec((rb, d), lambda i, gid, tid, nv, offs: (tid[i], 0)),
                  pl.BlockSpec((None, d, de), lambda i, gid, tid, nv, offs: (gid[i], 0, 0)),
                  pl.BlockSpec((None, d, de), lambda i, gid, tid, nv, offs: (gid[i], 0, 0)),
                  pl.BlockSpec((None, de, d), lambda i, gid, tid, nv, offs: (gid[i], 0, 0))],
        out_specs=pl.BlockSpec((rb, d), lambda i, gid, tid, nv, offs: (tid[i], 0)),
        scratch_shapes=[pltpu.VMEM((d, de), BF16), pltpu.VMEM((d, de), BF16),
                        pltpu.VMEM((de, d), BF16)],
    )
    return pl.pallas_call(
        _expert_kernel,
        grid_spec=grid_spec,
        out_shape=jax.ShapeDtypeStruct((n, d), F32),
        compiler_params=pltpu.CompilerParams(dimension_semantics=("arbitrary",),
                                             vmem_limit_bytes=VMEM_LIMIT),
        name="experts",
    )(gid, tid, nv, offs, buf, w_gate, w_up, w_down)


def _combine_kernel(dest_ref, x1_ref, p_ref, gw_ref, g3_ref, wpg_ref, wpp_ref, gf_ref, ybuf_ref,
                    out_ref, rows, sem):
    tt = x1_ref.shape[0]

    def issue(c, carry):
        for u in range(DMA_UNROLL):
            t = c * DMA_UNROLL + u
            for k in range(TOP_K):
                _row_copy(ybuf_ref, dest_ref[k, t], rows.at[k], t, sem).start(priority=k)
        return carry

    lax.fori_loop(0, tt // DMA_UNROLL, issue, 0)
    proj = _dot(p_ref[...].astype(BF16), wpp_ref[...])
    for k in range(TOP_K):
        pltpu.make_async_copy(ybuf_ref.at[pl.ds(0, tt), :], rows.at[k], sem).wait()

    x2 = x1_ref[...] + gw_ref[:, 0:1] * rows[0] + gw_ref[:, 1:2] * rows[1]
    hp = _rms(x2, g3_ref[...]).astype(BF16)
    x3 = x2 + _sigmoid(_dot(hp, wpg_ref[...])) * proj
    out_ref[...] = _rms(x3, gf_ref[...])


def _combine(x1, p, gw_t, dest, ybuf, g3, w_pg, w_pp, gf):
    t, d = x1.shape
    pd = p.shape[1]
    tt = COMBINE_TILE
    return pl.pallas_call(
        _combine_kernel,
        grid=(t // tt,),
        in_specs=[pl.BlockSpec((None, TOP_K, tt), lambda i: (i, 0, 0), memory_space=pltpu.SMEM),
                  pl.BlockSpec((tt, d), lambda i: (i, 0)),
                  pl.BlockSpec((tt, pd), lambda i: (i, 0)),
                  pl.BlockSpec((tt, TOP_K), lambda i: (i, 0)),
                  pl.BlockSpec((1, d), lambda i: (0, 0)),
                  pl.BlockSpec((d, d), lambda i: (0, 0)),
                  pl.BlockSpec((pd, d), lambda i: (0, 0)),
                  pl.BlockSpec((1, d), lambda i: (0, 0)),
                  pl.BlockSpec(memory_space=pl.ANY)],
        out_specs=pl.BlockSpec((tt, d), lambda i: (i, 0)),
        out_shape=jax.ShapeDtypeStruct((t, d), F32),
        scratch_shapes=[pltpu.VMEM((TOP_K, tt, d), F32), pltpu.SemaphoreType.DMA(())],
        compiler_params=pltpu.CompilerParams(dimension_semantics=("arbitrary",)),
        name="combine",
    )(dest.reshape(TOP_K, t // tt, tt).transpose(1, 0, 2), x1, p, gw_t, g3, w_pg, w_pp, gf, ybuf)


def _visit_plan(cnt, n_rows):
    rb = EXPERT_TILE
    offs = jnp.concatenate([jnp.zeros((1,), jnp.int32), jnp.cumsum(cnt, dtype=jnp.int32)])
    first = offs[:-1] // rb
    n_tiles = jnp.where(cnt > 0, (offs[1:] - 1) // rb - first + 1, 0)
    ends = jnp.cumsum(n_tiles, dtype=jnp.int32)
    nv = ends[-1]
    max_visits = n_rows // rb + N_EXPERTS - 1
    vi = jnp.minimum(jnp.arange(max_visits, dtype=jnp.int32), nv - 1)
    gid = jnp.minimum(jnp.sum((ends[None, :] <= vi[:, None]).astype(jnp.int32), axis=1),
                      N_EXPERTS - 1)
    tid = _lookup(first - (ends - n_tiles), gid) + vi
    return gid, tid, nv.reshape(1), offs


def _lookup(table, idx):
    ids = jnp.arange(table.shape[0], dtype=jnp.int32).reshape((-1,) + (1,) * idx.ndim)
    return jnp.sum(jnp.where(idx[None] == ids, table.reshape(ids.shape), 0), axis=0)


def _layer(x, p, mix_norm_g, w_in, conv_a_w, w_out_a, conv_b_w, conv_b_b, ln_b_g, ln_b_b, w_out_b,
           b_gate, w_o, ffn_norm_g, w_rg, b_rg, w_re, b_re, w_eg, w_eu, w_ed, ple_norm_g, w_pg, w_pp,
           out_norm_g):
    bsz, seq, d = x.shape
    t = bsz * seq
    row = lambda v: v.reshape(1, -1).astype(F32)
    w_r = jnp.zeros((ROUTER_ROWS, d), F32).at[0:N_GROUPS].set(w_rg.T).at[8:].set(w_re.T)
    b_r = jnp.zeros((ROUTER_ROWS,), F32).at[0:N_GROUPS].set(b_rg).at[8:].set(b_re)
    b_r = jnp.broadcast_to(b_r[:, None], (ROUTER_ROWS, LANES))

    x1, eid, gw, rank, cnt = _mixer(
        x, row(mix_norm_g), w_in.astype(BF16), conv_a_w, w_out_a.astype(BF16), conv_b_w,
        row(conv_b_b), row(ln_b_g), row(ln_b_b), w_out_b.astype(BF16), row(b_gate),
        w_o.astype(BF16), row(ffn_norm_g), w_r, b_r)
    x1 = x1.reshape(t, d)

    cnt = cnt[:, 0].astype(jnp.int32)
    gid, tid, nv, offs = _visit_plan(cnt, TOP_K * t)
    dest = _lookup(offs[:-1], eid) + rank

    buf = _scatter(x1, row(ffn_norm_g), dest)
    ybuf = _experts(buf, w_eg, w_eu, w_ed, gid, tid, nv, offs)
    out = _combine(x1, p.reshape(t, -1), gw.T, dest, ybuf, row(ple_norm_g), w_pg.astype(BF16),
                   w_pp.astype(BF16), row(out_norm_g))
    return out.reshape(bsz, seq, d)


def kernel(x, p, mix_norm_g, w_in, conv_a_w, w_out_a, conv_b_w, conv_b_b, ln_b_g, ln_b_b, w_out_b, b_gate, w_o, ffn_norm_g, w_router_group, b_router_group, w_router_expert, b_router_expert, w_exp_gate, w_exp_up, w_exp_down, ple_norm_g, w_ple_gate, w_ple_proj, final_norm_g):
    assert w_in.shape[0] == 1, "the combine kernel fuses the final RMSNorm: single layer only"
    return _layer(x, p[0], mix_norm_g[0], w_in[0], conv_a_w[0], w_out_a[0], conv_b_w[0],
                  conv_b_b[0], ln_b_g[0], ln_b_b[0], w_out_b[0], b_gate[0], w_o[0], ffn_norm_g[0],
                  w_router_group[0], b_router_group[0], w_router_expert[0], b_router_expert[0],
                  w_exp_gate[0], w_exp_up[0], w_exp_down[0], ple_norm_g[0], w_ple_gate[0],
                  w_ple_proj[0], final_norm_g)
```

```python
import functools

import jax
import jax.numpy as jnp
from jax import lax
from jax.experimental import pallas as pl
from jax.experimental.pallas import tpu as pltpu

F32 = jnp.float32
BF16 = jnp.bfloat16
EPS = 1e-6

LANES = 128
SUBLANES = 8
N_GROUPS = 4
EXPERTS_PER_GROUP = 8
N_EXPERTS = N_GROUPS * EXPERTS_PER_GROUP
TOP_K = 2
ROUTER_ROWS = 8 + N_EXPERTS

SEQ_TILE = 512
CONV_ROWS = 64
MXU_COLS = 256
HALO_A = 8
HALO_B = 32
TAIL_ROWS = 128
PIECE = SUBLANES
EXPERT_TILE = 256
PIECES_PER_BLOCK = EXPERT_TILE // PIECE
SORTED_ROWS = TOP_K * SEQ_TILE + N_EXPERTS * PIECE
VMEM_LIMIT = 56 * 1024 * 1024


def _sigmoid(v):
    return 1.0 / (1.0 + jnp.exp(-v))


def _rms(v, g):
    return v * lax.rsqrt(jnp.mean(v * v, axis=-1, keepdims=True) + EPS) * g


def _dot(a, b):
    return jnp.dot(a, b, preferred_element_type=F32)


def _conv_steps(buf, w_ref, width, halo, rows, emit):
    steps = []
    for j in range(buf.shape[0]):
        for base in range(0, rows, CONV_ROWS):
            def step(j=j, base=base):
                lanes = slice(j * LANES, (j + 1) * LANES)
                acc = None
                for k in range(width):
                    start = base + halo - (width - 1) + k
                    term = w_ref[k:k + 1, lanes] * buf[j, start:start + CONV_ROWS, :]
                    acc = term if acc is None else acc + term
                emit(slice(base, base + CONV_ROWS), lanes, acc)
            steps.append((width * CONV_ROWS // 8, step))
    return steps


def _interleave(mxu_steps, vpu_steps):
    total_m = sum(c for c, _ in mxu_steps)
    total_v = sum(c for c, _ in vpu_steps)
    done_m = 0
    done_v = 0
    pending = list(vpu_steps)
    for cost, step in mxu_steps:
        step()
        done_m += cost
        while pending and done_v * total_m < done_m * total_v:
            c, vstep = pending.pop(0)
            vstep()
            done_v += c
    for _, vstep in pending:
        vstep()


def _mixer_kernel(x_ref, g1_ref, win_ref, caw_ref, woa_ref, cbw_ref, cbb_ref, lng_ref, lnb_ref,
                  wob_ref, bg_ref, wo_ref, g2_ref, wr_ref, br_ref,
                  x1_ref, lrow_ref, gw_ref, np_ref,
                  abuf, bbuf, cbuf, pbuf, zbuf, swbuf, *, width_a, width_b, d_model, wa, wb):
    ts = x_ref.shape[0]

    @pl.when(pl.program_id(1) == 0)
    def _():
        abuf[:, 0:HALO_A, :] = jnp.zeros((abuf.shape[0], HALO_A, LANES), F32)
        bbuf[:, 0:HALO_B, :] = jnp.zeros((bbuf.shape[0], HALO_B, LANES), F32)

    h = _rms(x_ref[...], g1_ref[...]).astype(BF16)
    col_b, col_c, col_x = 0, wa, 2 * wa
    col_v, col_g, col_gate = 3 * wa, 3 * wa + wb, 3 * wa + 2 * wb

    def proj(col, q):
        return _dot(h, win_ref[:, col + q * MXU_COLS:col + (q + 1) * MXU_COLS])

    for q in range(wb // MXU_COLS):
        u = proj(col_v, q) * _sigmoid(proj(col_g, q))
        for jj in range(MXU_COLS // LANES):
            bbuf[q * (MXU_COLS // LANES) + jj, HALO_B:HALO_B + ts, :] = u[:, jj * LANES:(jj + 1) * LANES]

    raw_cols = ((col_c, wa), (col_x, wa), (col_b, wa), (col_gate, 2 * d_model))
    pb_c, pb_x, pb_b, pb_g = 0, wa, 2 * wa, 3 * wa
    mxu_steps = []
    off = 0
    for col, width in raw_cols:
        for q in range(width // MXU_COLS):
            def raw_step(col=col, q=q, off=off):
                lo = off + q * MXU_COLS
                pbuf[:, lo:lo + MXU_COLS] = proj(col, q)
            mxu_steps.append((1, raw_step))
        off += width

    def emit_b(rows, lanes, acc):
        cbuf[rows, lanes] = acc

    _interleave(mxu_steps, _conv_steps(bbuf, cbw_ref, width_b, HALO_B, ts, emit_b))
    bbuf[:, 0:HALO_B, :] = bbuf[:, ts:ts + HALO_B, :]

    for j in range(wa // LANES):
        abuf[j, HALO_A:HALO_A + ts, :] = (pbuf[:, pb_c + j * LANES:pb_c + (j + 1) * LANES]
                                           * pbuf[:, pb_x + j * LANES:pb_x + (j + 1) * LANES])

    def emit_a(rows, lanes, acc):
        zbuf[rows, lanes] = (pbuf[rows, pb_b + lanes.start:pb_b + lanes.stop] * acc).astype(BF16)

    for _, step in _conv_steps(abuf, caw_ref, width_a, HALO_A, ts, emit_a):
        step()
    abuf[:, 0:HALO_A, :] = abuf[:, ts:ts + HALO_A, :]

    for r in range(0, ts, TAIL_ROWS):
        rows = slice(r, r + TAIL_ROWS)
        c = cbuf[rows, :] + cbb_ref[...]
        mu = jnp.mean(c, axis=-1, keepdims=True)
        cc = c - mu
        var = jnp.mean(cc * cc, axis=-1, keepdims=True)
        ln = cc * lax.rsqrt(var + EPS) * lng_ref[...] + lnb_ref[...]
        swbuf[rows, :] = (ln * _sigmoid(ln)).astype(BF16)
        gcols = slice(pb_g, pb_g + 2 * d_model)
        pbuf[rows, gcols] = _sigmoid(pbuf[rows, gcols] + bg_ref[...])
    mix = (pbuf[:, pb_g:pb_g + d_model] * _dot(zbuf[...], woa_ref[...])
           + pbuf[:, pb_g + d_model:pb_g + 2 * d_model] * _dot(swbuf[...], wob_ref[...]))

    x1 = x_ref[...] + _dot(mix.astype(BF16), wo_ref[...])
    x1_ref[...] = x1

    h2 = _rms(x1, g2_ref[...])
    logits = lax.dot_general(wr_ref[...], h2, (((1,), (1,)), ((), ())),
                             preferred_element_type=F32) + br_ref[:, 0:1]
    gl = logits[0:N_GROUPS, :]
    gmax = jnp.max(gl, axis=0, keepdims=True)
    giota = lax.broadcasted_iota(jnp.int32, gl.shape, 0)
    grp = jnp.min(jnp.where(gl == gmax, giota, N_GROUPS), axis=0, keepdims=True)
    p_grp = 1.0 / jnp.sum(jnp.exp(gl - gmax), axis=0, keepdims=True)

    el = logits[8:8 + N_EXPERTS, :]
    eiota = lax.broadcasted_iota(jnp.int32, el.shape, 0)
    neg = jnp.float32(-jnp.inf)
    m0 = jnp.where(lax.div(eiota, EXPERTS_PER_GROUP) == grp, el, neg)
    v1 = jnp.max(m0, axis=0, keepdims=True)
    i1 = jnp.min(jnp.where(m0 == v1, eiota, N_EXPERTS), axis=0, keepdims=True)
    m1 = jnp.where(eiota == i1, neg, m0)
    v2 = jnp.max(m1, axis=0, keepdims=True)
    i2 = jnp.min(jnp.where(m1 == v2, eiota, N_EXPERTS), axis=0, keepdims=True)
    e2 = jnp.exp(v2 - v1)
    w1 = p_grp / (1.0 + e2)
    w2 = p_grp * e2 / (1.0 + e2)

    oh1 = eiota == i1
    oh2 = eiota == i2
    used = jnp.where(oh1, 1.0, jnp.where(oh2, 1.0, 0.0))
    rio = lax.broadcasted_iota(jnp.int32, (ts, ts), 0)
    cio = lax.broadcasted_iota(jnp.int32, (ts, ts), 1)
    before = jnp.where(rio < cio, 1.0, 0.0).astype(BF16)
    in_strip = _dot(used.astype(BF16), before)
    pieces = jnp.floor((jnp.sum(used, axis=1, keepdims=True) + (PIECE - 1.0)) * (1.0 / PIECE))
    er = lax.broadcasted_iota(jnp.int32, (N_EXPERTS, N_EXPERTS), 0)
    ec = lax.broadcasted_iota(jnp.int32, (N_EXPERTS, N_EXPERTS), 1)
    earlier = jnp.where(ec < er, 1.0, 0.0)
    strip_start = _dot(earlier, jnp.broadcast_to(pieces * PIECE, (N_EXPERTS, LANES)))[:, 0:1]
    row = strip_start + in_strip
    r1 = jnp.sum(jnp.where(oh1, row, 0.0), axis=0, keepdims=True)
    r2 = jnp.sum(jnp.where(oh2, row, 0.0), axis=0, keepdims=True)

    lrow_ref[...] = jnp.concatenate([r1, r2], axis=0).astype(jnp.int32)
    gw_ref[...] = jnp.concatenate([w1, w2], axis=0)
    np_ref[...] = jnp.broadcast_to(pieces, np_ref.shape).astype(jnp.int32)


def _const_spec(shape):
    return pl.BlockSpec(shape, lambda *_: (0,) * len(shape), pipeline_mode=pl.Buffered(1))


def _mixer(x, g1, w_in, conv_a_w, w_out_a, conv_b_w, conv_b_b, ln_g, ln_b, w_out_b, b_gate, w_o,
           g2, w_r, b_r):
    bsz, seq, d = x.shape
    wa = w_out_a.shape[0]
    wb = w_out_b.shape[0]
    ts = SEQ_TILE
    n_s = seq // ts
    t = bsz * seq
    consts = (g1, w_in, conv_a_w, w_out_a, conv_b_w, conv_b_b, ln_g, ln_b, w_out_b, b_gate, w_o,
              g2, w_r, b_r)
    tok_spec = pl.BlockSpec((TOP_K, ts), lambda b, s: (0, b * n_s + s))
    return pl.pallas_call(
        functools.partial(_mixer_kernel, width_a=conv_a_w.shape[0], width_b=conv_b_w.shape[0],
                          d_model=d, wa=wa, wb=wb),
        grid=(bsz, n_s),
        in_specs=[pl.BlockSpec((None, ts, d), lambda b, s: (b, s, 0))]
                 + [_const_spec(c.shape) for c in consts],
        out_specs=[pl.BlockSpec((None, ts, d), lambda b, s: (b, s, 0)),
                   tok_spec, tok_spec,
                   pl.BlockSpec((None, N_EXPERTS, LANES), lambda b, s: (b * n_s + s, 0, 0))],
        out_shape=[jax.ShapeDtypeStruct((bsz, seq, d), F32),
                   jax.ShapeDtypeStruct((TOP_K, t), jnp.int32),
                   jax.ShapeDtypeStruct((TOP_K, t), F32),
                   jax.ShapeDtypeStruct((t // ts, N_EXPERTS, LANES), jnp.int32)],
        scratch_shapes=[pltpu.VMEM((wa // LANES, HALO_A + ts, LANES), F32),
                        pltpu.VMEM((wb // LANES, HALO_B + ts, LANES), F32),
                        pltpu.VMEM((ts, wb), F32),
                        pltpu.VMEM((ts, 3 * wa + 2 * d), F32),
                        pltpu.VMEM((ts, wa), BF16),
                        pltpu.VMEM((ts, wb), BF16)],
        compiler_params=pltpu.CompilerParams(
            dimension_semantics=("arbitrary", "arbitrary"), vmem_limit_bytes=VMEM_LIMIT),
        name="mixer",
    )(x, *consts)


def _sort_kernel(lrow_ref, x1_ref, g2_ref, hs_ref):
    ts = x1_ref.shape[0]
    h2 = _rms(x1_ref[...], g2_ref[...]).astype(BF16)
    riota = lax.broadcasted_iota(jnp.int32, (hs_ref.shape[0], ts), 0)
    pick = jnp.where(riota == lrow_ref[0:1, :], 1.0, jnp.where(riota == lrow_ref[1:2, :], 1.0, 0.0))
    hs_ref[...] = _dot(pick.astype(BF16), h2)


def _sort(x1, g2, lrow):
    t, d = x1.shape
    ts = SEQ_TILE
    return pl.pallas_call(
        _sort_kernel,
        grid=(t // ts,),
        in_specs=[pl.BlockSpec((TOP_K, ts), lambda i: (0, i)),
                  pl.BlockSpec((ts, d), lambda i: (i, 0)),
                  pl.BlockSpec((1, d), lambda i: (0, 0))],
        out_specs=pl.BlockSpec((SORTED_ROWS, d), lambda i: (i, 0)),
        out_shape=jax.ShapeDtypeStruct((t // ts * SORTED_ROWS, d), F32),
        compiler_params=pltpu.CompilerParams(dimension_semantics=("arbitrary",),
                                             vmem_limit_bytes=VMEM_LIMIT),
        name="sort",
    )(lrow, x1, g2)


def _zero_unwritten(used_ref, ys_ref, zero, sem):
    n_tiles = used_ref.shape[0]
    rb = zero.shape[0]

    def fill(row, n_rows):
        return pltpu.make_async_copy(zero.at[pl.ds(0, n_rows), :],
                                     ys_ref.at[pl.ds(row, n_rows), :], sem)

    dump = [fill(r, rb) for r in range(n_tiles * SORTED_ROWS, ys_ref.shape[0], rb)]

    def tails(wait):
        def body(i, carry):
            used = used_ref[i]
            tail = SORTED_ROWS - used
            row = i * SORTED_ROWS + used
            size = rb
            while size >= PIECE:
                @pl.when((tail & size) != 0)
                def _(row=row, size=size):
                    c = fill(pl.multiple_of(row, PIECE), size)
                    c.wait() if wait else c.start()
                row = row + (tail & size)
                size //= 2
            return carry
        lax.fori_loop(0, n_tiles, body, 0)

    zero[...] = jnp.zeros(zero.shape, F32)
    for c in dump:
        c.start()
    tails(False)
    for c in dump:
        c.wait()
    tails(True)


def _expert_kernel(bexp_ref, nused_ref, src_ref, dst_ref, used_ref, hs_ref, wg_ref, wu_ref, wd_ref,
                   ys_ref, xbuf, ybuf, wg_bf, wu_bf, wd_bf, sem_in, sem_out):
    b = pl.program_id(0)
    n_used = nused_ref[0]
    rb = xbuf.shape[1]

    @pl.when(b == 0)
    def _():
        _zero_unwritten(used_ref, ys_ref, ybuf.at[1], sem_out.at[1])

    def piece_copies(table_ref, blk, hbm_ref, buf, slot, sem, inbound):
        copies = []
        for j in range(PIECES_PER_BLOCK):
            row = pl.multiple_of(table_ref[blk * PIECES_PER_BLOCK + j] * PIECE, PIECE)
            hbm = hbm_ref.at[pl.ds(row, PIECE), :]
            vmem = buf.at[slot, pl.ds(j * PIECE, PIECE), :]
            copies.append(pltpu.make_async_copy(hbm, vmem, sem.at[slot]) if inbound
                          else pltpu.make_async_copy(vmem, hbm, sem.at[slot]))
        return copies

    def start_in(blk, slot):
        for c in piece_copies(src_ref, blk, hs_ref, xbuf, slot, sem_in, True):
            c.start()

    def start_out(blk, slot):
        for c in piece_copies(dst_ref, blk, ys_ref, ybuf, slot, sem_out, False):
            c.start()

    def wait_in(slot):
        pltpu.make_async_copy(hs_ref.at[pl.ds(0, rb), :], xbuf.at[slot], sem_in.at[slot]).wait()

    def wait_out(slot):
        pltpu.make_async_copy(ybuf.at[slot], ys_ref.at[pl.ds(0, rb), :], sem_out.at[slot]).wait()

    @pl.when(b < n_used)
    def _():
        slot = lax.rem(b, 2)
        other = 1 - slot
        prev = jnp.maximum(b - 1, 0)

        @pl.when(b == 0)
        def _():
            start_in(0, 0)

        @pl.when(b + 1 < n_used)
        def _():
            start_in(b + 1, other)

        @pl.when((b == 0) | (bexp_ref[prev] != bexp_ref[b]))
        def _():
            wg_bf[...] = wg_ref[...].astype(BF16)
            wu_bf[...] = wu_ref[...].astype(BF16)
            wd_bf[...] = wd_ref[...].astype(BF16)

        wait_in(slot)

        @pl.when(b >= 2)
        def _():
            wait_out(slot)

        xb = xbuf[slot].astype(BF16)
        a = _dot(xb, wg_bf[...])
        hid = (a * _sigmoid(a) * _dot(xb, wu_bf[...])).astype(BF16)
        ybuf[slot] = _dot(hid, wd_bf[...])
        start_out(b, slot)

        @pl.when(b == n_used - 1)
        def _():
            @pl.when(b >= 1)
            def _():
                wait_out(other)
            wait_out(slot)


def _experts(hs, w_gate, w_up, w_down, bexp, nused, src, dst, used_rows, n_out_rows):
    _, d = hs.shape
    _, _, de = w_gate.shape
    rb = EXPERT_TILE
    assert SORTED_ROWS - TOP_K * SEQ_TILE <= rb and (n_out_rows - hs.shape[0]) % rb == 0
    wmap = lambda b, bexp, nused, src, dst, used: (bexp[b], 0, 0)
    grid_spec = pltpu.PrefetchScalarGridSpec(
        num_scalar_prefetch=5,
        grid=(bexp.shape[0],),
        in_specs=[pl.BlockSpec(memory_space=pl.ANY),
                  pl.BlockSpec((None, d, de), wmap),
                  pl.BlockSpec((None, d, de), wmap),
                  pl.BlockSpec((None, de, d), wmap)],
        out_specs=pl.BlockSpec(memory_space=pl.ANY),
        scratch_shapes=[pltpu.VMEM((2, rb, d), F32), pltpu.VMEM((2, rb, d), F32),
                        pltpu.VMEM((d, de), BF16), pltpu.VMEM((d, de), BF16),
                        pltpu.VMEM((de, d), BF16),
                        pltpu.SemaphoreType.DMA((2,)), pltpu.SemaphoreType.DMA((2,))],
    )
    return pl.pallas_call(
        _expert_kernel,
        grid_spec=grid_spec,
        out_shape=jax.ShapeDtypeStruct((n_out_rows, d), F32),
        compiler_params=pltpu.CompilerParams(dimension_semantics=("arbitrary",),
                                             vmem_limit_bytes=VMEM_LIMIT),
        name="experts",
    )(bexp, nused, src, dst, used_rows, hs, w_gate, w_up, w_down)


def _combine_kernel(x1_ref, p_ref, lrow_ref, gw_ref, g3_ref, wpg_ref, wpp_ref, gf_ref, ys_ref,
                    out_ref):
    ts = x1_ref.shape[0]
    ciota = lax.broadcasted_iota(jnp.int32, (ts, ys_ref.shape[0]), 1)
    pick = jnp.where(ciota == lrow_ref[:, 0:1], gw_ref[:, 0:1],
                     jnp.where(ciota == lrow_ref[:, 1:2], gw_ref[:, 1:2], 0.0))
    x2 = x1_ref[...] + _dot(pick.astype(BF16), ys_ref[...].astype(BF16))
    hp = _rms(x2, g3_ref[...]).astype(BF16)
    proj = _dot(p_ref[...].astype(BF16), wpp_ref[...])
    x3 = x2 + _sigmoid(_dot(hp, wpg_ref[...])) * proj
    out_ref[...] = _rms(x3, gf_ref[...])


def _combine(x1, p, lrow_t, gw_t, ys, g3, w_pg, w_pp, gf):
    t, d = x1.shape
    pd = p.shape[1]
    ts = SEQ_TILE
    tile = lambda i: (i, 0)
    const = lambda i: (0, 0)
    return pl.pallas_call(
        _combine_kernel,
        grid=(t // ts,),
        in_specs=[pl.BlockSpec((ts, d), tile),
                  pl.BlockSpec((ts, pd), tile),
                  pl.BlockSpec((ts, TOP_K), tile),
                  pl.BlockSpec((ts, TOP_K), tile),
                  pl.BlockSpec((1, d), const),
                  pl.BlockSpec((d, d), const),
                  pl.BlockSpec((pd, d), const),
                  pl.BlockSpec((1, d), const),
                  pl.BlockSpec((SORTED_ROWS, d), tile)],
        out_specs=pl.BlockSpec((ts, d), tile),
        out_shape=jax.ShapeDtypeStruct((t, d), F32),
        compiler_params=pltpu.CompilerParams(dimension_semantics=("arbitrary",),
                                             vmem_limit_bytes=VMEM_LIMIT),
        name="combine",
    )(x1, p, lrow_t, gw_t, g3, w_pg, w_pp, gf, ys)


def _piece_plan(n_pieces):
    n_tiles = n_pieces.shape[0]
    ppb = PIECES_PER_BLOCK
    tile_pieces = SORTED_ROWS // PIECE
    i32 = jnp.int32
    excl = lambda v, axis: jnp.cumsum(v, axis=axis, dtype=i32) - v
    base = (jnp.arange(n_tiles, dtype=i32) * tile_pieces)[:, None] + excl(n_pieces, 1)
    per_expert = n_pieces.T
    blocks = (jnp.sum(per_expert, axis=1, dtype=i32) + ppb - 1) // ppb
    first_block = excl(blocks, 0)
    n_used = jnp.sum(blocks, dtype=i32)
    strip_slot = (first_block[:, None] * ppb + excl(per_expert, 1)).reshape(-1)
    strip_len = per_expert.reshape(-1)
    strip_base = base.T.reshape(-1)

    max_blocks = (TOP_K * n_tiles * SEQ_TILE // PIECE + n_tiles * N_EXPERTS) // ppb + N_EXPERTS
    slots = jnp.arange(max_blocks * ppb, dtype=i32)[:, None]
    inside = (slots >= strip_slot[None, :]) & (slots < (strip_slot + strip_len)[None, :])
    src = jnp.sum(jnp.where(inside, strip_base[None, :] + slots - strip_slot[None, :], 0), axis=1,
                  dtype=i32)
    real = jnp.any(inside, axis=1)
    n_dump = N_EXPERTS * ppb
    pad_rank = jnp.minimum(jnp.cumsum(jnp.logical_not(real), dtype=i32) - 1, n_dump - 1)
    dst = jnp.where(real, src, n_tiles * tile_pieces + pad_rank)
    blk = jnp.minimum(jnp.arange(max_blocks, dtype=i32), n_used - 1)
    bexp = jnp.sum((first_block[None, :] <= blk[:, None]).astype(i32), axis=1) - 1
    n_out_rows = (n_tiles * tile_pieces + n_dump) * PIECE
    return bexp, n_used.reshape(1), src, dst, n_out_rows


def _layer(x, p, mix_norm_g, w_in, conv_a_w, w_out_a, conv_b_w, conv_b_b, ln_b_g, ln_b_b, w_out_b,
           b_gate, w_o, ffn_norm_g, w_rg, b_rg, w_re, b_re, w_eg, w_eu, w_ed, ple_norm_g, w_pg, w_pp,
           out_norm_g):
    bsz, seq, d = x.shape
    t = bsz * seq
    row = lambda v: v.reshape(1, -1).astype(F32)
    w_r = jnp.zeros((ROUTER_ROWS, d), F32).at[0:N_GROUPS].set(w_rg.T).at[8:].set(w_re.T)
    b_r = jnp.zeros((ROUTER_ROWS,), F32).at[0:N_GROUPS].set(b_rg).at[8:].set(b_re)
    b_r = jnp.broadcast_to(b_r[:, None], (ROUTER_ROWS, LANES))

    x1, lrow, gw, n_pieces = _mixer(
        x, row(mix_norm_g), w_in.astype(BF16), conv_a_w, w_out_a.astype(BF16), conv_b_w,
        row(conv_b_b), row(ln_b_g), row(ln_b_b), w_out_b.astype(BF16), row(b_gate),
        w_o.astype(BF16), row(ffn_norm_g), w_r, b_r)
    x1 = x1.reshape(t, d)

    n_pieces = n_pieces[:, :, 0]
    bexp, n_used, src, dst, n_out_rows = _piece_plan(n_pieces)
    used_rows = jnp.sum(n_pieces, axis=1, dtype=jnp.int32) * PIECE
    hs = _sort(x1, row(ffn_norm_g), lrow)
    ys = _experts(hs, w_eg, w_eu, w_ed, bexp, n_used, src, dst, used_rows, n_out_rows)
    out = _combine(x1, p.reshape(t, -1), lrow.T, gw.T, ys, row(ple_norm_g),
                   w_pg.astype(BF16), w_pp.astype(BF16), row(out_norm_g))
    return out.reshape(bsz, seq, d)


def kernel(x, p, mix_norm_g, w_in, conv_a_w, w_out_a, conv_b_w, conv_b_b, ln_b_g, ln_b_b, w_out_b, b_gate, w_o, ffn_norm_g, w_router_group, b_router_group, w_router_expert, b_router_expert, w_exp_gate, w_exp_up, w_exp_down, ple_norm_g, w_ple_gate, w_ple_proj, final_norm_g):
    assert w_in.shape[0] == 1, "the combine kernel fuses the final RMSNorm: single layer only"
    return _layer(x, p[0], mix_norm_g[0], w_in[0], conv_a_w[0], w_out_a[0], conv_b_w[0],
                  conv_b_b[0], ln_b_g[0], ln_b_b[0], w_out_b[0], b_gate[0], w_o[0], ffn_norm_g[0],
                  w_router_group[0], b_router_group[0], w_router_expert[0], b_router_expert[0],
                  w_exp_gate[0], w_exp_up[0], w_exp_down[0], ple_norm_g[0], w_ple_gate[0],
                  w_ple_proj[0], final_norm_g)
```

```python
import functools

import jax
import jax.numpy as jnp
from jax import lax
from jax.experimental import pallas as pl
from jax.experimental.pallas import tpu as pltpu

F32 = jnp.float32
BF16 = jnp.bfloat16
EPS = 1e-6

LANES = 128
SUBLANES = 8
N_GROUPS = 4
EXPERTS_PER_GROUP = 8
N_EXPERTS = N_GROUPS * EXPERTS_PER_GROUP
TOP_K = 2
ROUTER_ROWS = 8 + N_EXPERTS

SEQ_TILE = 512
CONV_ROWS = 64
MXU_COLS = 256
PROJ_PER_ITER = 4
HALO_A = 8
HALO_B = 32
TAIL_ROWS = 128
PIECE = SUBLANES
EXPERT_TILE = 256
PIECES_PER_BLOCK = EXPERT_TILE // PIECE
SORTED_ROWS = TOP_K * SEQ_TILE + N_EXPERTS * PIECE
VMEM_LIMIT = 56 * 1024 * 1024
MIXER_VMEM_LIMIT = 60 * 1024 * 1024


def _sigmoid(v):
    return 1.0 / (1.0 + jnp.exp(-v))


def _rms(v, g):
    return v * lax.rsqrt(jnp.mean(v * v, axis=-1, keepdims=True) + EPS) * g


def _dot(a, b):
    return jnp.dot(a, b, preferred_element_type=F32)


def _conv_chunk(buf, w_ref, j, base, width, halo):
    acc = None
    for k in range(width):
        term = w_ref[j, k:k + 1, :] * buf[j, pl.ds(base + (halo - (width - 1) + k), CONV_ROWS), :]
        acc = term if acc is None else acc + term
    return acc


def _mixer_kernel(x_ref, g1_ref, wrest_ref, wloop_ref, caw_ref, woa_ref, cbw_ref, cbb_ref, lng_ref,
                  lnb_ref, wob_ref, bg_ref, wo_ref, g2_ref, wr_ref, br_ref,
                  x1_ref, lrow_ref, gw_ref, np_ref,
                  hbuf, abuf, bbuf, cbuf, pbuf, abb, zbuf, swbuf, *, width_a, width_b, d_model, wa, wb):
    ts = x_ref.shape[0]
    n_lane_a = wa // LANES
    n_lane_b = wb // LANES
    lanes_per_step = MXU_COLS // LANES

    @pl.when(pl.program_id(1) == 0)
    def _():
        abuf[:, 0:HALO_A, :] = jnp.zeros((n_lane_a, HALO_A, LANES), F32)
        bbuf[:, 0:HALO_B, :] = jnp.zeros((n_lane_b, HALO_B, LANES), F32)

    hbuf[...] = _rms(x_ref[...], g1_ref[...]).astype(BF16)

    def rest(col, q):
        return _dot(hbuf[...], wrest_ref[:, col + q * MXU_COLS:col + (q + 1) * MXU_COLS])

    for q in range(wb // MXU_COLS):
        u = rest(wa, q) * _sigmoid(rest(wa + wb, q))
        for jj in range(lanes_per_step):
            bbuf[q * lanes_per_step + jj, HALO_B:HALO_B + ts, :] = u[:, jj * LANES:(jj + 1) * LANES]

    n_iter = wloop_ref.shape[0] // PROJ_PER_ITER
    chunks_per_lane = ts // CONV_ROWS
    lanes_per_iter = n_lane_b // n_iter

    def step(i, carry):
        for s in range(PROJ_PER_ITER):
            pbuf[i * PROJ_PER_ITER + s] = _dot(hbuf[...], wloop_ref[i * PROJ_PER_ITER + s])
        for jj in range(lanes_per_iter):
            j = i * lanes_per_iter + jj
            for c in range(chunks_per_lane):
                base = c * CONV_ROWS
                cbuf[j, base:base + CONV_ROWS, :] = _conv_chunk(bbuf, cbw_ref, j, base, width_b, HALO_B)
        return carry

    lax.fori_loop(0, n_iter, step, 0)
    bbuf[:, 0:HALO_B, :] = bbuf[:, ts:ts + HALO_B, :]

    pc, px, pg = 0, wa // MXU_COLS, 2 * wa // MXU_COLS
    for j in range(n_lane_a):
        q = j // lanes_per_step
        lanes = slice((j % lanes_per_step) * LANES, (j % lanes_per_step + 1) * LANES)
        abuf[j, HALO_A:HALO_A + ts, :] = pbuf[pc + q, :, lanes] * pbuf[px + q, :, lanes]
    for q in range(wa // MXU_COLS):
        abb[:, q * MXU_COLS:(q + 1) * MXU_COLS] = rest(0, q)
    for j in range(n_lane_a):
        lanes = slice(j * LANES, (j + 1) * LANES)
        for base in range(0, ts, CONV_ROWS):
            rows = slice(base, base + CONV_ROWS)
            acc = _conv_chunk(abuf, caw_ref, j, base, width_a, HALO_A)
            zbuf[rows, lanes] = (abb[rows, lanes] * acc).astype(BF16)
    abuf[:, 0:HALO_A, :] = abuf[:, ts:ts + HALO_A, :]

    for r in range(0, ts, TAIL_ROWS):
        rows = slice(r, r + TAIL_ROWS)
        c = jnp.concatenate([cbuf[j, rows, :] for j in range(n_lane_b)], axis=1) + cbb_ref[...]
        mu = jnp.mean(c, axis=-1, keepdims=True)
        cc = c - mu
        var = jnp.mean(cc * cc, axis=-1, keepdims=True)
        ln = cc * lax.rsqrt(var + EPS) * lng_ref[...] + lnb_ref[...]
        swbuf[rows, :] = (ln * _sigmoid(ln)).astype(BF16)
        for q in range(2 * d_model // MXU_COLS):
            pbuf[pg + q, rows, :] = _sigmoid(pbuf[pg + q, rows, :]
                                             + bg_ref[:, q * MXU_COLS:(q + 1) * MXU_COLS])
    n_gate = d_model // MXU_COLS
    gate_a = jnp.concatenate([pbuf[pg + q] for q in range(n_gate)], axis=1)
    gate_b = jnp.concatenate([pbuf[pg + n_gate + q] for q in range(n_gate)], axis=1)
    mix = gate_a * _dot(zbuf[...], woa_ref[...]) + gate_b * _dot(swbuf[...], wob_ref[...])

    x1 = x_ref[...] + _dot(mix.astype(BF16), wo_ref[...])
    x1_ref[...] = x1

    h2 = _rms(x1, g2_ref[...])
    logits = lax.dot_general(wr_ref[...], h2, (((1,), (1,)), ((), ())),
                             preferred_element_type=F32) + br_ref[:, 0:1]
    gl = logits[0:N_GROUPS, :]
    gmax = jnp.max(gl, axis=0, keepdims=True)
    giota = lax.broadcasted_iota(jnp.int32, gl.shape, 0)
    grp = jnp.min(jnp.where(gl == gmax, giota, N_GROUPS), axis=0, keepdims=True)
    p_grp = 1.0 / jnp.sum(jnp.exp(gl - gmax), axis=0, keepdims=True)

    el = logits[8:8 + N_EXPERTS, :]
    eiota = lax.broadcasted_iota(jnp.int32, el.shape, 0)
    neg = jnp.float32(-jnp.inf)
    m0 = jnp.where(lax.div(eiota, EXPERTS_PER_GROUP) == grp, el, neg)
    v1 = jnp.max(m0, axis=0, keepdims=True)
    i1 = jnp.min(jnp.where(m0 == v1, eiota, N_EXPERTS), axis=0, keepdims=True)
    m1 = jnp.where(eiota == i1, neg, m0)
    v2 = jnp.max(m1, axis=0, keepdims=True)
    i2 = jnp.min(jnp.where(m1 == v2, eiota, N_EXPERTS), axis=0, keepdims=True)
    e2 = jnp.exp(v2 - v1)
    w1 = p_grp / (1.0 + e2)
    w2 = p_grp * e2 / (1.0 + e2)

    oh1 = eiota == i1
    oh2 = eiota == i2
    used = jnp.where(oh1, 1.0, jnp.where(oh2, 1.0, 0.0))
    rio = lax.broadcasted_iota(jnp.int32, (ts, ts), 0)
    cio = lax.broadcasted_iota(jnp.int32, (ts, ts), 1)
    before = jnp.where(rio < cio, 1.0, 0.0).astype(BF16)
    in_strip = _dot(used.astype(BF16), before)
    pieces = jnp.floor((jnp.sum(used, axis=1, keepdims=True) + (PIECE - 1.0)) * (1.0 / PIECE))
    er = lax.broadcasted_iota(jnp.int32, (N_EXPERTS, N_EXPERTS), 0)
    ec = lax.broadcasted_iota(jnp.int32, (N_EXPERTS, N_EXPERTS), 1)
    earlier = jnp.where(ec < er, 1.0, 0.0)
    strip_start = _dot(earlier, jnp.broadcast_to(pieces * PIECE, (N_EXPERTS, LANES)))[:, 0:1]
    row = strip_start + in_strip
    r1 = jnp.sum(jnp.where(oh1, row, 0.0), axis=0, keepdims=True)
    r2 = jnp.sum(jnp.where(oh2, row, 0.0), axis=0, keepdims=True)

    lrow_ref[...] = jnp.concatenate([r1, r2], axis=0).astype(jnp.int32)
    gw_ref[...] = jnp.concatenate([w1, w2], axis=0)
    np_ref[...] = jnp.broadcast_to(pieces, np_ref.shape).astype(jnp.int32)


def _const_spec(shape):
    return pl.BlockSpec(shape, lambda *_: (0,) * len(shape), pipeline_mode=pl.Buffered(1))


def _lane_chunks(w):
    k, c = w.shape
    return w.reshape(k, c // LANES, LANES).transpose(1, 0, 2)


def _mixer(x, g1, w_in, conv_a_w, w_out_a, conv_b_w, conv_b_b, ln_g, ln_b, w_out_b, b_gate, w_o,
           g2, w_r, b_r):
    bsz, seq, d = x.shape
    wa = w_out_a.shape[0]
    wb = w_out_b.shape[0]
    ts = SEQ_TILE
    n_s = seq // ts
    t = bsz * seq
    w_rest = jnp.concatenate([w_in[:, 0:wa], w_in[:, 3 * wa:3 * wa + 2 * wb]], axis=1)
    w_loop = jnp.concatenate([w_in[:, wa:3 * wa], w_in[:, 3 * wa + 2 * wb:]], axis=1)
    w_loop = w_loop.reshape(d, -1, MXU_COLS).transpose(1, 0, 2)
    consts = (g1, w_rest, w_loop, _lane_chunks(conv_a_w), w_out_a, _lane_chunks(conv_b_w), conv_b_b,
              ln_g, ln_b, w_out_b, b_gate, w_o, g2, w_r, b_r)
    tok_spec = pl.BlockSpec((TOP_K, ts), lambda b, s: (0, b * n_s + s))
    return pl.pallas_call(
        functools.partial(_mixer_kernel, width_a=conv_a_w.shape[0], width_b=conv_b_w.shape[0],
                          d_model=d, wa=wa, wb=wb),
        grid=(bsz, n_s),
        in_specs=[pl.BlockSpec((None, ts, d), lambda b, s: (b, s, 0))]
                 + [_const_spec(c.shape) for c in consts],
        out_specs=[pl.BlockSpec((None, ts, d), lambda b, s: (b, s, 0)),
                   tok_spec, tok_spec,
                   pl.BlockSpec((None, N_EXPERTS, LANES), lambda b, s: (b * n_s + s, 0, 0))],
        out_shape=[jax.ShapeDtypeStruct((bsz, seq, d), F32),
                   jax.ShapeDtypeStruct((TOP_K, t), jnp.int32),
                   jax.ShapeDtypeStruct((TOP_K, t), F32),
                   jax.ShapeDtypeStruct((t // ts, N_EXPERTS, LANES), jnp.int32)],
        scratch_shapes=[pltpu.VMEM((ts, d), BF16),
                        pltpu.VMEM((wa // LANES, HALO_A + ts, LANES), F32),
                        pltpu.VMEM((wb // LANES, HALO_B + ts, LANES), F32),
                        pltpu.VMEM((wb // LANES, ts, LANES), F32),
                        pltpu.VMEM((w_loop.shape[0], ts, MXU_COLS), F32),
                        pltpu.VMEM((ts, wa), F32),
                        pltpu.VMEM((ts, wa), BF16),
                        pltpu.VMEM((ts, wb), BF16)],
        compiler_params=pltpu.CompilerParams(
            dimension_semantics=("arbitrary", "arbitrary"), vmem_limit_bytes=MIXER_VMEM_LIMIT),
        name="mixer",
    )(x, *consts)


def _sort_kernel(lrow_ref, x1_ref, g2_ref, hs_ref):
    ts = x1_ref.shape[0]
    h2 = _rms(x1_ref[...], g2_ref[...]).astype(BF16)
    riota = lax.broadcasted_iota(jnp.int32, (hs_ref.shape[0], ts), 0)
    pick = jnp.where(riota == lrow_ref[0:1, :], 1.0, jnp.where(riota == lrow_ref[1:2, :], 1.0, 0.0))
    hs_ref[...] = _dot(pick.astype(BF16), h2)


def _sort(x1, g2, lrow):
    t, d = x1.shape
    ts = SEQ_TILE
    return pl.pallas_call(
        _sort_kernel,
        grid=(t // ts,),
        in_specs=[pl.BlockSpec((TOP_K, ts), lambda i: (0, i)),
                  pl.BlockSpec((ts, d), lambda i: (i, 0)),
                  pl.BlockSpec((1, d), lambda i: (0, 0))],
        out_specs=pl.BlockSpec((SORTED_ROWS, d), lambda i: (i, 0)),
        out_shape=jax.ShapeDtypeStruct((t // ts * SORTED_ROWS, d), F32),
        compiler_params=pltpu.CompilerParams(dimension_semantics=("arbitrary",),
                                             vmem_limit_bytes=VMEM_LIMIT),
        name="sort",
    )(lrow, x1, g2)


def _zero_unwritten(used_ref, ys_ref, zero, sem):
    n_tiles = used_ref.shape[0]
    rb = zero.shape[0]

    def fill(row, n_rows):
        return pltpu.make_async_copy(zero.at[pl.ds(0, n_rows), :],
                                     ys_ref.at[pl.ds(row, n_rows), :], sem)

    dump = [fill(r, rb) for r in range(n_tiles * SORTED_ROWS, ys_ref.shape[0], rb)]

    def tails(wait):
        def body(i, carry):
            used = used_ref[i]
            tail = SORTED_ROWS - used
            row = i * SORTED_ROWS + used
            size = rb
            while size >= PIECE:
                @pl.when((tail & size) != 0)
                def _(row=row, size=size):
                    c = fill(pl.multiple_of(row, PIECE), size)
                    c.wait() if wait else c.start()
                row = row + (tail & size)
                size //= 2
            return carry
        lax.fori_loop(0, n_tiles, body, 0)

    zero[...] = jnp.zeros(zero.shape, F32)
    for c in dump:
        c.start()
    tails(False)
    for c in dump:
        c.wait()
    tails(True)


def _expert_kernel(bexp_ref, nused_ref, src_ref, dst_ref, used_ref, hs_ref, wg_ref, wu_ref, wd_ref,
                   ys_ref, xbuf, ybuf, wg_bf, wu_bf, wd_bf, sem_in, sem_out):
    b = pl.program_id(0)
    n_used = nused_ref[0]
    rb = xbuf.shape[1]

    @pl.when(b == 0)
    def _():
        _zero_unwritten(used_ref, ys_ref, ybuf.at[1], sem_out.at[1])

    def piece_copies(table_ref, blk, hbm_ref, buf, slot, sem, inbound):
        copies = []
        for j in range(PIECES_PER_BLOCK):
            row = pl.multiple_of(table_ref[blk * PIECES_PER_BLOCK + j] * PIECE, PIECE)
            hbm = hbm_ref.at[pl.ds(row, PIECE), :]
            vmem = buf.at[slot, pl.ds(j * PIECE, PIECE), :]
            copies.append(pltpu.make_async_copy(hbm, vmem, sem.at[slot]) if inbound
                          else pltpu.make_async_copy(vmem, hbm, sem.at[slot]))
        return copies

    def start_in(blk, slot):
        for c in piece_copies(src_ref, blk, hs_ref, xbuf, slot, sem_in, True):
            c.start()

    def start_out(blk, slot):
        for c in piece_copies(dst_ref, blk, ys_ref, ybuf, slot, sem_out, False):
            c.start()

    def wait_in(slot):
        pltpu.make_async_copy(hs_ref.at[pl.ds(0, rb), :], xbuf.at[slot], sem_in.at[slot]).wait()

    def wait_out(slot):
        pltpu.make_async_copy(ybuf.at[slot], ys_ref.at[pl.ds(0, rb), :], sem_out.at[slot]).wait()

    @pl.when(b < n_used)
    def _():
        slot = lax.rem(b, 2)
        other = 1 - slot
        prev = jnp.maximum(b - 1, 0)

        @pl.when(b == 0)
        def _():
            start_in(0, 0)

        @pl.when((b == 0) | (bexp_ref[prev] != bexp_ref[b]))
        def _():
            wg_bf[...] = wg_ref[...].astype(BF16)
            wu_bf[...] = wu_ref[...].astype(BF16)
            wd_bf[...] = wd_ref[...].astype(BF16)

        wait_in(slot)

        @pl.when(b >= 2)
        def _():
            wait_out(slot)

        start_in(jnp.minimum(b + 1, n_used - 1), other)
        xb = xbuf[slot].astype(BF16)
        a = _dot(xb, wg_bf[...])
        hid = (a * _sigmoid(a) * _dot(xb, wu_bf[...])).astype(BF16)
        ybuf[slot] = _dot(hid, wd_bf[...])
        start_out(b, slot)

        @pl.when(b == n_used - 1)
        def _():
            wait_in(other)

            @pl.when(b >= 1)
            def _():
                wait_out(other)
            wait_out(slot)


def _experts(hs, w_gate, w_up, w_down, bexp, nused, src, dst, used_rows, n_out_rows):
    _, d = hs.shape
    _, _, de = w_gate.shape
    rb = EXPERT_TILE
    assert SORTED_ROWS - TOP_K * SEQ_TILE <= rb and (n_out_rows - hs.shape[0]) % rb == 0
    wmap = lambda b, bexp, nused, src, dst, used: (bexp[b], 0, 0)
    grid_spec = pltpu.PrefetchScalarGridSpec(
        num_scalar_prefetch=5,
        grid=(bexp.shape[0],),
        in_specs=[pl.BlockSpec(memory_space=pl.ANY),
                  pl.BlockSpec((None, d, de), wmap),
                  pl.BlockSpec((None, d, de), wmap),
                  pl.BlockSpec((None, de, d), wmap)],
        out_specs=pl.BlockSpec(memory_space=pl.ANY),
        scratch_shapes=[pltpu.VMEM((2, rb, d), F32), pltpu.VMEM((2, rb, d), F32),
                        pltpu.VMEM((d, de), BF16), pltpu.VMEM((d, de), BF16),
                        pltpu.VMEM((de, d), BF16),
                        pltpu.SemaphoreType.DMA((2,)), pltpu.SemaphoreType.DMA((2,))],
    )
    return pl.pallas_call(
        _expert_kernel,
        grid_spec=grid_spec,
        out_shape=jax.ShapeDtypeStruct((n_out_rows, d), F32),
        compiler_params=pltpu.CompilerParams(dimension_semantics=("arbitrary",),
                                             vmem_limit_bytes=VMEM_LIMIT),
        name="experts",
    )(bexp, nused, src, dst, used_rows, hs, w_gate, w_up, w_down)


def _combine_kernel(x1_ref, p_ref, lrow_ref, gw_ref, g3_ref, wpg_ref, wpp_ref, gf_ref, ys_ref,
                    out_ref):
    ts = x1_ref.shape[0]
    ciota = lax.broadcasted_iota(jnp.int32, (ts, ys_ref.shape[0]), 1)
    pick = jnp.where(ciota == lrow_ref[:, 0:1], gw_ref[:, 0:1],
                     jnp.where(ciota == lrow_ref[:, 1:2], gw_ref[:, 1:2], 0.0))
    x2 = x1_ref[...] + _dot(pick.astype(BF16), ys_ref[...].astype(BF16))
    hp = _rms(x2, g3_ref[...]).astype(BF16)
    proj = _dot(p_ref[...].astype(BF16), wpp_ref[...])
    x3 = x2 + _sigmoid(_dot(hp, wpg_ref[...])) * proj
    out_ref[...] = _rms(x3, gf_ref[...])


def _combine(x1, p, lrow_t, gw_t, ys, g3, w_pg, w_pp, gf):
    t, d = x1.shape
    pd = p.shape[1]
    ts = SEQ_TILE
    tile = lambda i: (i, 0)
    const = lambda i: (0, 0)
    return pl.pallas_call(
        _combine_kernel,
        grid=(t // ts,),
        in_specs=[pl.BlockSpec((ts, d), tile),
                  pl.BlockSpec((ts, pd), tile),
                  pl.BlockSpec((ts, TOP_K), tile),
                  pl.BlockSpec((ts, TOP_K), tile),
                  pl.BlockSpec((1, d), const),
                  pl.BlockSpec((d, d), const),
                  pl.BlockSpec((pd, d), const),
                  pl.BlockSpec((1, d), const),
                  pl.BlockSpec((SORTED_ROWS, d), tile)],
        out_specs=pl.BlockSpec((ts, d), tile),
        out_shape=jax.ShapeDtypeStruct((t, d), F32),
        compiler_params=pltpu.CompilerParams(dimension_semantics=("arbitrary",),
                                             vmem_limit_bytes=VMEM_LIMIT),
        name="combine",
    )(x1, p, lrow_t, gw_t, g3, w_pg, w_pp, gf, ys)


def _piece_plan(n_pieces):
    n_tiles = n_pieces.shape[0]
    ppb = PIECES_PER_BLOCK
    tile_pieces = SORTED_ROWS // PIECE
    i32 = jnp.int32
    excl = lambda v, axis: jnp.cumsum(v, axis=axis, dtype=i32) - v
    base = (jnp.arange(n_tiles, dtype=i32) * tile_pieces)[:, None] + excl(n_pieces, 1)
    per_expert = n_pieces.T
    blocks = (jnp.sum(per_expert, axis=1, dtype=i32) + ppb - 1) // ppb
    first_block = excl(blocks, 0)
    n_used = jnp.sum(blocks, dtype=i32)
    strip_slot = (first_block[:, None] * ppb + excl(per_expert, 1)).reshape(-1)
    strip_len = per_expert.reshape(-1)
    strip_base = base.T.reshape(-1)

    max_blocks = (TOP_K * n_tiles * SEQ_TILE // PIECE + n_tiles * N_EXPERTS) // ppb + N_EXPERTS
    slots = jnp.arange(max_blocks * ppb, dtype=i32)[:, None]
    inside = (slots >= strip_slot[None, :]) & (slots < (strip_slot + strip_len)[None, :])
    src = jnp.sum(jnp.where(inside, strip_base[None, :] + slots - strip_slot[None, :], 0), axis=1,
                  dtype=i32)
    real = jnp.any(inside, axis=1)
    n_dump = N_EXPERTS * ppb
    pad_rank = jnp.minimum(jnp.cumsum(jnp.logical_not(real), dtype=i32) - 1, n_dump - 1)
    dst = jnp.where(real, src, n_tiles * tile_pieces + pad_rank)
    blk = jnp.minimum(jnp.arange(max_blocks, dtype=i32), n_used - 1)
    bexp = jnp.sum((first_block[None, :] <= blk[:, None]).astype(i32), axis=1) - 1
    n_out_rows = (n_tiles * tile_pieces + n_dump) * PIECE
    return bexp, n_used.reshape(1), src, dst, n_out_rows


def _layer(x, p, mix_norm_g, w_in, conv_a_w, w_out_a, conv_b_w, conv_b_b, ln_b_g, ln_b_b, w_out_b,
           b_gate, w_o, ffn_norm_g, w_rg, b_rg, w_re, b_re, w_eg, w_eu, w_ed, ple_norm_g, w_pg, w_pp,
           out_norm_g):
    bsz, seq, d = x.shape
    t = bsz * seq
    row = lambda v: v.reshape(1, -1).astype(F32)
    w_r = jnp.zeros((ROUTER_ROWS, d), F32).at[0:N_GROUPS].set(w_rg.T).at[8:].set(w_re.T)
    b_r = jnp.zeros((ROUTER_ROWS,), F32).at[0:N_GROUPS].set(b_rg).at[8:].set(b_re)
    b_r = jnp.broadcast_to(b_r[:, None], (ROUTER_ROWS, LANES))

    x1, lrow, gw, n_pieces = _mixer(
        x, row(mix_norm_g), w_in.astype(BF16), conv_a_w, w_out_a.astype(BF16), conv_b_w,
        row(conv_b_b), row(ln_b_g), row(ln_b_b), w_out_b.astype(BF16), row(b_gate),
        w_o.astype(BF16), row(ffn_norm_g), w_r, b_r)
    x1 = x1.reshape(t, d)

    n_pieces = n_pieces[:, :, 0]
    bexp, n_used, src, dst, n_out_rows = _piece_plan(n_pieces)
    used_rows = jnp.sum(n_pieces, axis=1, dtype=jnp.int32) * PIECE
    hs = _sort(x1, row(ffn_norm_g), lrow)
    ys = _experts(hs, w_eg, w_eu, w_ed, bexp, n_used, src, dst, used_rows, n_out_rows)
    out = _combine(x1, p.reshape(t, -1), lrow.T, gw.T, ys, row(ple_norm_g),
                   w_pg.astype(BF16), w_pp.astype(BF16), row(out_norm_g))
    return out.reshape(bsz, seq, d)


def kernel(x, p, mix_norm_g, w_in, conv_a_w, w_out_a, conv_b_w, conv_b_b, ln_b_g, ln_b_b, w_out_b, b_gate, w_o, ffn_norm_g, w_router_group, b_router_group, w_router_expert, b_router_expert, w_exp_gate, w_exp_up, w_exp_down, ple_norm_g, w_ple_gate, w_ple_proj, final_norm_g):
    assert w_in.shape[0] == 1, "the combine kernel fuses the final RMSNorm: single layer only"
    return _layer(x, p[0], mix_norm_g[0], w_in[0], conv_a_w[0], w_out_a[0], conv_b_w[0],
                  conv_b_b[0], ln_b_g[0], ln_b_b[0], w_out_b[0], b_gate[0], w_o[0], ffn_norm_g[0],
                  w_router_group[0], b_router_group[0], w_router_expert[0], b_router_expert[0],
                  w_exp_gate[0], w_exp_up[0], w_exp_down[0], ple_norm_g[0], w_ple_gate[0],
                  w_ple_proj[0], final_norm_g)
```

```python
import functools

import jax
import jax.numpy as jnp
from jax import lax
from jax.experimental import pallas as pl
from jax.experimental.pallas import tpu as pltpu

F32 = jnp.float32
BF16 = jnp.bfloat16
EPS = 1e-6

LANES = 128
SUBLANES = 8
N_GROUPS = 4
EXPERTS_PER_GROUP = 8
N_EXPERTS = N_GROUPS * EXPERTS_PER_GROUP
TOP_K = 2
ROUTER_ROWS = 8 + N_EXPERTS

SEQ_TILE = 512
CONV_ROWS = 64
MXU_COLS = 256
HALO_A = 8
HALO_B = 32
TAIL_ROWS = 128
PIECE = SUBLANES
EXPERT_TILE = 512
PIECES_PER_BLOCK = EXPERT_TILE // PIECE
SORTED_ROWS = TOP_K * SEQ_TILE + N_EXPERTS * PIECE
VMEM_LIMIT = 56 * 1024 * 1024
MIXER_VMEM_LIMIT = 60 * 1024 * 1024


def _sigmoid(v):
    return 1.0 / (1.0 + jnp.exp(-v))


def _rms(v, g):
    return v * lax.rsqrt(jnp.mean(v * v, axis=-1, keepdims=True) + EPS) * g


def _dot(a, b):
    return jnp.dot(a, b, preferred_element_type=F32)


def _conv_steps(buf, w_ref, width, halo, rows, emit):
    steps = []
    for j in range(buf.shape[0]):
        for base in range(0, rows, CONV_ROWS):
            def step(j=j, base=base):
                lanes = slice(j * LANES, (j + 1) * LANES)
                acc = None
                for k in range(width):
                    start = base + halo - (width - 1) + k
                    term = w_ref[k:k + 1, lanes] * buf[j, start:start + CONV_ROWS, :]
                    acc = term if acc is None else acc + term
                emit(slice(base, base + CONV_ROWS), lanes, acc)
            steps.append((width * CONV_ROWS // 8, step))
    return steps


def _interleave(mxu_steps, vpu_steps):
    total_m = sum(c for c, _ in mxu_steps)
    total_v = sum(c for c, _ in vpu_steps)
    done_m = 0
    done_v = 0
    pending = list(vpu_steps)
    for cost, step in mxu_steps:
        step()
        done_m += cost
        while pending and done_v * total_m < done_m * total_v:
            c, vstep = pending.pop(0)
            vstep()
            done_v += c
    for _, vstep in pending:
        vstep()


def _mixer_kernel(x_ref, g1_ref, win_ref, caw_ref, woa_ref, cbw_ref, cbb_ref, lng_ref, lnb_ref,
                  wob_ref, bg_ref, wo_ref, g2_ref, wr_ref, br_ref,
                  x1_ref, lrow_ref, gw_ref, np_ref, hs_ref,
                  abuf, bbuf, cbuf, pbuf, zbuf, swbuf, *, width_a, width_b, d_model, wa, wb):
    ts = x_ref.shape[0]

    @pl.when(pl.program_id(1) == 0)
    def _():
        abuf[:, 0:HALO_A, :] = jnp.zeros((abuf.shape[0], HALO_A, LANES), F32)
        bbuf[:, 0:HALO_B, :] = jnp.zeros((bbuf.shape[0], HALO_B, LANES), F32)

    h = _rms(x_ref[...], g1_ref[...]).astype(BF16)
    col_b, col_c, col_x = 0, wa, 2 * wa
    col_v, col_g, col_gate = 3 * wa, 3 * wa + wb, 3 * wa + 2 * wb

    def proj(col, q):
        return _dot(h, win_ref[:, col + q * MXU_COLS:col + (q + 1) * MXU_COLS])

    for q in range(wb // MXU_COLS):
        u = proj(col_v, q) * _sigmoid(proj(col_g, q))
        for jj in range(MXU_COLS // LANES):
            bbuf[q * (MXU_COLS // LANES) + jj, HALO_B:HALO_B + ts, :] = u[:, jj * LANES:(jj + 1) * LANES]

    raw_cols = ((col_c, wa), (col_x, wa), (col_b, wa), (col_gate, 2 * d_model))
    pb_c, pb_x, pb_b, pb_g = 0, wa, 2 * wa, 3 * wa
    mxu_steps = []
    off = 0
    for col, width in raw_cols:
        for q in range(width // MXU_COLS):
            def raw_step(col=col, q=q, off=off):
                lo = off + q * MXU_COLS
                pbuf[:, lo:lo + MXU_COLS] = proj(col, q)
            mxu_steps.append((1, raw_step))
        off += width

    def emit_b(rows, lanes, acc):
        cbuf[rows, lanes] = acc

    _interleave(mxu_steps, _conv_steps(bbuf, cbw_ref, width_b, HALO_B, ts, emit_b))
    bbuf[:, 0:HALO_B, :] = bbuf[:, ts:ts + HALO_B, :]

    for j in range(wa // LANES):
        abuf[j, HALO_A:HALO_A + ts, :] = (pbuf[:, pb_c + j * LANES:pb_c + (j + 1) * LANES]
                                           * pbuf[:, pb_x + j * LANES:pb_x + (j + 1) * LANES])

    def emit_a(rows, lanes, acc):
        zbuf[rows, lanes] = (pbuf[rows, pb_b + lanes.start:pb_b + lanes.stop] * acc).astype(BF16)

    for _, step in _conv_steps(abuf, caw_ref, width_a, HALO_A, ts, emit_a):
        step()
    abuf[:, 0:HALO_A, :] = abuf[:, ts:ts + HALO_A, :]

    for r in range(0, ts, TAIL_ROWS):
        rows = slice(r, r + TAIL_ROWS)
        c = cbuf[rows, :] + cbb_ref[...]
        mu = jnp.mean(c, axis=-1, keepdims=True)
        cc = c - mu
        var = jnp.mean(cc * cc, axis=-1, keepdims=True)
        ln = cc * lax.rsqrt(var + EPS) * lng_ref[...] + lnb_ref[...]
        swbuf[rows, :] = (ln * _sigmoid(ln)).astype(BF16)
        gcols = slice(pb_g, pb_g + 2 * d_model)
        pbuf[rows, gcols] = _sigmoid(pbuf[rows, gcols] + bg_ref[...])
    mix = (pbuf[:, pb_g:pb_g + d_model] * _dot(zbuf[...], woa_ref[...])
           + pbuf[:, pb_g + d_model:pb_g + 2 * d_model] * _dot(swbuf[...], wob_ref[...]))

    x1 = x_ref[...] + _dot(mix.astype(BF16), wo_ref[...])
    x1_ref[...] = x1

    h2 = _rms(x1, g2_ref[...])
    logits = lax.dot_general(wr_ref[...], h2, (((1,), (1,)), ((), ())),
                             preferred_element_type=F32) + br_ref[:, 0:1]
    gl = logits[0:N_GROUPS, :]
    gmax = jnp.max(gl, axis=0, keepdims=True)
    giota = lax.broadcasted_iota(jnp.int32, gl.shape, 0)
    grp = jnp.min(jnp.where(gl == gmax, giota, N_GROUPS), axis=0, keepdims=True)
    p_grp = 1.0 / jnp.sum(jnp.exp(gl - gmax), axis=0, keepdims=True)

    el = logits[8:8 + N_EXPERTS, :]
    eiota = lax.broadcasted_iota(jnp.int32, el.shape, 0)
    neg = jnp.float32(-jnp.inf)
    m0 = jnp.where(lax.div(eiota, EXPERTS_PER_GROUP) == grp, el, neg)
    v1 = jnp.max(m0, axis=0, keepdims=True)
    i1 = jnp.min(jnp.where(m0 == v1, eiota, N_EXPERTS), axis=0, keepdims=True)
    m1 = jnp.where(eiota == i1, neg, m0)
    v2 = jnp.max(m1, axis=0, keepdims=True)
    i2 = jnp.min(jnp.where(m1 == v2, eiota, N_EXPERTS), axis=0, keepdims=True)
    e2 = jnp.exp(v2 - v1)
    w1 = p_grp / (1.0 + e2)
    w2 = p_grp * e2 / (1.0 + e2)

    oh1 = eiota == i1
    oh2 = eiota == i2
    used = jnp.where(oh1, 1.0, jnp.where(oh2, 1.0, 0.0))
    rio = lax.broadcasted_iota(jnp.int32, (ts, ts), 0)
    cio = lax.broadcasted_iota(jnp.int32, (ts, ts), 1)
    before = jnp.where(rio < cio, 1.0, 0.0).astype(BF16)
    in_strip = _dot(used.astype(BF16), before)
    pieces = jnp.floor((jnp.sum(used, axis=1, keepdims=True) + (PIECE - 1.0)) * (1.0 / PIECE))
    er = lax.broadcasted_iota(jnp.int32, (N_EXPERTS, N_EXPERTS), 0)
    ec = lax.broadcasted_iota(jnp.int32, (N_EXPERTS, N_EXPERTS), 1)
    earlier = jnp.where(ec < er, 1.0, 0.0)
    strip_start = _dot(earlier, jnp.broadcast_to(pieces * PIECE, (N_EXPERTS, LANES)))[:, 0:1]
    row = strip_start + in_strip
    r1 = jnp.sum(jnp.where(oh1, row, 0.0), axis=0, keepdims=True)
    r2 = jnp.sum(jnp.where(oh2, row, 0.0), axis=0, keepdims=True)

    lrow = jnp.concatenate([r1, r2], axis=0).astype(jnp.int32)
    lrow_ref[...] = lrow
    gw_ref[...] = jnp.concatenate([w1, w2], axis=0)
    np_ref[...] = jnp.broadcast_to(pieces, np_ref.shape).astype(jnp.int32)

    riota = lax.broadcasted_iota(jnp.int32, (hs_ref.shape[0], ts), 0)
    pick = jnp.where(riota == lrow[0:1, :], 1.0, jnp.where(riota == lrow[1:2, :], 1.0, 0.0))
    hs_ref[...] = _dot(pick.astype(BF16), h2.astype(BF16))


def _const_spec(shape):
    return pl.BlockSpec(shape, lambda *_: (0,) * len(shape), pipeline_mode=pl.Buffered(1))


def _mixer(x, g1, w_in, conv_a_w, w_out_a, conv_b_w, conv_b_b, ln_g, ln_b, w_out_b, b_gate, w_o,
           g2, w_r, b_r):
    bsz, seq, d = x.shape
    wa = w_out_a.shape[0]
    wb = w_out_b.shape[0]
    ts = SEQ_TILE
    n_s = seq // ts
    t = bsz * seq
    consts = (g1, w_in, conv_a_w, w_out_a, conv_b_w, conv_b_b, ln_g, ln_b, w_out_b, b_gate, w_o,
              g2, w_r, b_r)
    tok_spec = pl.BlockSpec((TOP_K, ts), lambda b, s: (0, b * n_s + s))
    return pl.pallas_call(
        functools.partial(_mixer_kernel, width_a=conv_a_w.shape[0], width_b=conv_b_w.shape[0],
                          d_model=d, wa=wa, wb=wb),
        grid=(bsz, n_s),
        in_specs=[pl.BlockSpec((None, ts, d), lambda b, s: (b, s, 0))]
                 + [_const_spec(c.shape) for c in consts],
        out_specs=[pl.BlockSpec((None, ts, d), lambda b, s: (b, s, 0)),
                   tok_spec, tok_spec,
                   pl.BlockSpec((None, N_EXPERTS, LANES), lambda b, s: (b * n_s + s, 0, 0)),
                   pl.BlockSpec((SORTED_ROWS, d), lambda b, s: (b * n_s + s, 0))],
        out_shape=[jax.ShapeDtypeStruct((bsz, seq, d), F32),
                   jax.ShapeDtypeStruct((TOP_K, t), jnp.int32),
                   jax.ShapeDtypeStruct((TOP_K, t), F32),
                   jax.ShapeDtypeStruct((t // ts, N_EXPERTS, LANES), jnp.int32),
                   jax.ShapeDtypeStruct((t // ts * SORTED_ROWS, d), F32)],
        scratch_shapes=[pltpu.VMEM((wa // LANES, HALO_A + ts, LANES), F32),
                        pltpu.VMEM((wb // LANES, HALO_B + ts, LANES), F32),
                        pltpu.VMEM((ts, wb), F32),
                        pltpu.VMEM((ts, 3 * wa + 2 * d), F32),
                        pltpu.VMEM((ts, wa), BF16),
                        pltpu.VMEM((ts, wb), BF16)],
        compiler_params=pltpu.CompilerParams(
            dimension_semantics=("arbitrary", "arbitrary"), vmem_limit_bytes=MIXER_VMEM_LIMIT),
        name="mixer",
    )(x, *consts)


def _zero_unwritten(used_ref, ys_ref, zero, sem):
    n_tiles = used_ref.shape[0]
    rb = zero.shape[0]

    def fill(row, n_rows):
        return pltpu.make_async_copy(zero.at[pl.ds(0, n_rows), :],
                                     ys_ref.at[pl.ds(row, n_rows), :], sem)

    dump = [fill(r, rb) for r in range(n_tiles * SORTED_ROWS, ys_ref.shape[0], rb)]

    def tails(wait):
        def body(i, carry):
            used = used_ref[i]
            tail = SORTED_ROWS - used
            row = i * SORTED_ROWS + used
            size = SORTED_ROWS - TOP_K * SEQ_TILE
            while size >= PIECE:
                @pl.when((tail & size) != 0)
                def _(row=row, size=size):
                    c = fill(pl.multiple_of(row, PIECE), size)
                    c.wait() if wait else c.start()
                row = row + (tail & size)
                size //= 2
            return carry
        lax.fori_loop(0, n_tiles, body, 0)

    zero[...] = jnp.zeros(zero.shape, F32)
    for c in dump:
        c.start()
    tails(False)
    for c in dump:
        c.wait()
    tails(True)


def _expert_kernel(bexp_ref, nused_ref, src_ref, dst_ref, used_ref, hs_ref, wg_ref, wu_ref, wd_ref,
                   ys_ref, xbuf, ybuf, wg_bf, wu_bf, wd_bf, sem_in, sem_out):
    b = pl.program_id(0)
    n_used = nused_ref[0]
    rb = xbuf.shape[1]

    @pl.when(b == 0)
    def _():
        _zero_unwritten(used_ref, ys_ref, ybuf.at[1], sem_out.at[1])

    def piece_copies(table_ref, blk, hbm_ref, buf, slot, sem, inbound):
        copies = []
        for j in range(PIECES_PER_BLOCK):
            row = pl.multiple_of(table_ref[blk * PIECES_PER_BLOCK + j] * PIECE, PIECE)
            hbm = hbm_ref.at[pl.ds(row, PIECE), :]
            vmem = buf.at[slot, pl.ds(j * PIECE, PIECE), :]
            copies.append(pltpu.make_async_copy(hbm, vmem, sem.at[slot]) if inbound
                          else pltpu.make_async_copy(vmem, hbm, sem.at[slot]))
        return copies

    def start_in(blk, slot):
        for c in piece_copies(src_ref, blk, hs_ref, xbuf, slot, sem_in, True):
            c.start()

    def start_out(blk, slot):
        for c in piece_copies(dst_ref, blk, ys_ref, ybuf, slot, sem_out, False):
            c.start()

    def wait_in(slot):
        pltpu.make_async_copy(hs_ref.at[pl.ds(0, rb), :], xbuf.at[slot], sem_in.at[slot]).wait()

    def wait_out(slot):
        pltpu.make_async_copy(ybuf.at[slot], ys_ref.at[pl.ds(0, rb), :], sem_out.at[slot]).wait()

    @pl.when(b < n_used)
    def _():
        slot = lax.rem(b, 2)
        other = 1 - slot
        prev = jnp.maximum(b - 1, 0)

        @pl.when(b == 0)
        def _():
            start_in(0, 0)

        @pl.when(b + 1 < n_used)
        def _():
            start_in(b + 1, other)

        @pl.when((b == 0) | (bexp_ref[prev] != bexp_ref[b]))
        def _():
            wg_bf[...] = wg_ref[...].astype(BF16)
            wu_bf[...] = wu_ref[...].astype(BF16)
            wd_bf[...] = wd_ref[...].astype(BF16)

        wait_in(slot)

        @pl.when(b >= 2)
        def _():
            wait_out(slot)

        xb = xbuf[slot].astype(BF16)
        a = _dot(xb, wg_bf[...])
        hid = (a * _sigmoid(a) * _dot(xb, wu_bf[...])).astype(BF16)
        ybuf[slot] = _dot(hid, wd_bf[...])
        start_out(b, slot)

        @pl.when(b == n_used - 1)
        def _():
            @pl.when(b >= 1)
            def _():
                wait_out(other)
            wait_out(slot)


def _experts(hs, w_gate, w_up, w_down, bexp, nused, src, dst, used_rows, n_out_rows):
    _, d = hs.shape
    _, _, de = w_gate.shape
    rb = EXPERT_TILE
    assert (n_out_rows - hs.shape[0]) % rb == 0
    wmap = lambda b, bexp, nused, src, dst, used: (bexp[b], 0, 0)
    grid_spec = pltpu.PrefetchScalarGridSpec(
        num_scalar_prefetch=5,
        grid=(bexp.shape[0],),
        in_specs=[pl.BlockSpec(memory_space=pl.ANY),
                  pl.BlockSpec((None, d, de), wmap),
                  pl.BlockSpec((None, d, de), wmap),
                  pl.BlockSpec((None, de, d), wmap)],
        out_specs=pl.BlockSpec(memory_space=pl.ANY),
        scratch_shapes=[pltpu.VMEM((2, rb, d), F32), pltpu.VMEM((2, rb, d), F32),
                        pltpu.VMEM((d, de), BF16), pltpu.VMEM((d, de), BF16),
                        pltpu.VMEM((de, d), BF16),
                        pltpu.SemaphoreType.DMA((2,)), pltpu.SemaphoreType.DMA((2,))],
    )
    return pl.pallas_call(
        _expert_kernel,
        grid_spec=grid_spec,
        out_shape=jax.ShapeDtypeStruct((n_out_rows, d), F32),
        compiler_params=pltpu.CompilerParams(dimension_semantics=("arbitrary",),
                                             vmem_limit_bytes=VMEM_LIMIT),
        name="experts",
    )(bexp, nused, src, dst, used_rows, hs, w_gate, w_up, w_down)


def _combine_kernel(x1_ref, p_ref, lrow_ref, gw_ref, g3_ref, wpg_ref, wpp_ref, gf_ref, ys_ref,
                    out_ref):
    ts = x1_ref.shape[0]
    ciota = lax.broadcasted_iota(jnp.int32, (ts, ys_ref.shape[0]), 1)
    pick = jnp.where(ciota == lrow_ref[:, 0:1], gw_ref[:, 0:1],
                     jnp.where(ciota == lrow_ref[:, 1:2], gw_ref[:, 1:2], 0.0))
    x2 = x1_ref[...] + _dot(pick.astype(BF16), ys_ref[...].astype(BF16))
    hp = _rms(x2, g3_ref[...]).astype(BF16)
    proj = _dot(p_ref[...].astype(BF16), wpp_ref[...])
    x3 = x2 + _sigmoid(_dot(hp, wpg_ref[...])) * proj
    out_ref[...] = _rms(x3, gf_ref[...])


def _combine(x1, p, lrow_t, gw_t, ys, g3, w_pg, w_pp, gf):
    t, d = x1.shape
    pd = p.shape[1]
    ts = SEQ_TILE
    tile = lambda i: (i, 0)
    const = lambda i: (0, 0)
    return pl.pallas_call(
        _combine_kernel,
        grid=(t // ts,),
        in_specs=[pl.BlockSpec((ts, d), tile),
                  pl.BlockSpec((ts, pd), tile),
                  pl.BlockSpec((ts, TOP_K), tile),
                  pl.BlockSpec((ts, TOP_K), tile),
                  pl.BlockSpec((1, d), const),
                  pl.BlockSpec((d, d), const),
                  pl.BlockSpec((pd, d), const),
                  pl.BlockSpec((1, d), const),
                  pl.BlockSpec((SORTED_ROWS, d), tile)],
        out_specs=pl.BlockSpec((ts, d), tile),
        out_shape=jax.ShapeDtypeStruct((t, d), F32),
        compiler_params=pltpu.CompilerParams(dimension_semantics=("arbitrary",),
                                             vmem_limit_bytes=VMEM_LIMIT),
        name="combine",
    )(x1, p, lrow_t, gw_t, g3, w_pg, w_pp, gf, ys)


def _piece_plan(n_pieces):
    n_tiles = n_pieces.shape[0]
    ppb = PIECES_PER_BLOCK
    tile_pieces = SORTED_ROWS // PIECE
    i32 = jnp.int32
    excl = lambda v, axis: jnp.cumsum(v, axis=axis, dtype=i32) - v
    base = (jnp.arange(n_tiles, dtype=i32) * tile_pieces)[:, None] + excl(n_pieces, 1)
    per_expert = n_pieces.T
    blocks = (jnp.sum(per_expert, axis=1, dtype=i32) + ppb - 1) // ppb
    first_block = excl(blocks, 0)
    n_used = jnp.sum(blocks, dtype=i32)

    max_blocks = (TOP_K * n_tiles * SEQ_TILE // PIECE + n_tiles * N_EXPERTS) // ppb + N_EXPERTS
    block_ids = jnp.arange(max_blocks, dtype=i32)
    blk = jnp.minimum(block_ids, n_used - 1)
    bexp = jnp.sum((first_block[None, :] <= blk[:, None]).astype(i32), axis=1) - 1
    strip_lo = jnp.take(excl(per_expert, 1), bexp, axis=0)[:, None, :]
    strip_len = jnp.take(per_expert, bexp, axis=0)[:, None, :]
    strip_base = jnp.take(base.T, bexp, axis=0)[:, None, :]
    local = ((block_ids - jnp.take(first_block, bexp)) * ppb)[:, None] + jnp.arange(ppb, dtype=i32)
    local = local[:, :, None]
    inside = (local >= strip_lo) & (local < strip_lo + strip_len)
    src = jnp.sum(jnp.where(inside, strip_base + local - strip_lo, 0), axis=2, dtype=i32).reshape(-1)
    real = jnp.any(inside, axis=2).reshape(-1)
    n_dump = N_EXPERTS * ppb
    pad_rank = jnp.minimum(jnp.cumsum(jnp.logical_not(real), dtype=i32) - 1, n_dump - 1)
    dst = jnp.where(real, src, n_tiles * tile_pieces + pad_rank)
    n_out_rows = (n_tiles * tile_pieces + n_dump) * PIECE
    return bexp, n_used.reshape(1), src, dst, n_out_rows


def _layer(x, p, mix_norm_g, w_in, conv_a_w, w_out_a, conv_b_w, conv_b_b, ln_b_g, ln_b_b, w_out_b,
           b_gate, w_o, ffn_norm_g, w_rg, b_rg, w_re, b_re, w_eg, w_eu, w_ed, ple_norm_g, w_pg, w_pp,
           out_norm_g):
    bsz, seq, d = x.shape
    t = bsz * seq
    row = lambda v: v.reshape(1, -1).astype(F32)
    w_r = jnp.zeros((ROUTER_ROWS, d), F32).at[0:N_GROUPS].set(w_rg.T).at[8:].set(w_re.T)
    b_r = jnp.zeros((ROUTER_ROWS,), F32).at[0:N_GROUPS].set(b_rg).at[8:].set(b_re)
    b_r = jnp.broadcast_to(b_r[:, None], (ROUTER_ROWS, LANES))

    x1, lrow, gw, n_pieces, hs = _mixer(
        x, row(mix_norm_g), w_in.astype(BF16), conv_a_w, w_out_a.astype(BF16), conv_b_w,
        row(conv_b_b), row(ln_b_g), row(ln_b_b), w_out_b.astype(BF16), row(b_gate),
        w_o.astype(BF16), row(ffn_norm_g), w_r, b_r)
    x1 = x1.reshape(t, d)

    n_pieces = n_pieces[:, :, 0]
    bexp, n_used, src, dst, n_out_rows = _piece_plan(n_pieces)
    used_rows = jnp.sum(n_pieces, axis=1, dtype=jnp.int32) * PIECE
    ys = _experts(hs, w_eg, w_eu, w_ed, bexp, n_used, src, dst, used_rows, n_out_rows)
    out = _combine(x1, p.reshape(t, -1), lrow.T, gw.T, ys, row(ple_norm_g),
                   w_pg.astype(BF16), w_pp.astype(BF16), row(out_norm_g))
    return out.reshape(bsz, seq, d)


def kernel(x, p, mix_norm_g, w_in, conv_a_w, w_out_a, conv_b_w, conv_b_b, ln_b_g, ln_b_b, w_out_b, b_gate, w_o, ffn_norm_g, w_router_group, b_router_group, w_router_expert, b_router_expert, w_exp_gate, w_exp_up, w_exp_down, ple_norm_g, w_ple_gate, w_ple_proj, final_norm_g):
    assert w_in.shape[0] == 1, "the combine kernel fuses the final RMSNorm: single layer only"
    return _layer(x, p[0], mix_norm_g[0], w_in[0], conv_a_w[0], w_out_a[0], conv_b_w[0],
                  conv_b_b[0], ln_b_g[0], ln_b_b[0], w_out_b[0], b_gate[0], w_o[0], ffn_norm_g[0],
                  w_router_group[0], b_router_group[0], w_router_expert[0], b_router_expert[0],
                  w_exp_gate[0], w_exp_up[0], w_exp_down[0], ple_norm_g[0], w_ple_gate[0],
                  w_ple_proj[0], final_norm_g)
```

```python
import functools

import jax
import jax.numpy as jnp
from jax import lax
from jax.experimental import pallas as pl
from jax.experimental.pallas import tpu as pltpu

F32 = jnp.float32
BF16 = jnp.bfloat16
EPS = 1e-6

LANES = 128
SUBLANES = 8
N_GROUPS = 4
EXPERTS_PER_GROUP = 8
N_EXPERTS = N_GROUPS * EXPERTS_PER_GROUP
TOP_K = 2
ROUTER_ROWS = 8 + N_EXPERTS

SEQ_TILE = 512
CONV_ROWS = 128
MXU_COLS = 256
HALO_A = 8
HALO_B = 32
TAIL_ROWS = 128
PIECE = SUBLANES
EXPERT_TILE = 512
PIECES_PER_BLOCK = EXPERT_TILE // PIECE
SORTED_ROWS = TOP_K * SEQ_TILE + N_EXPERTS * PIECE
VMEM_LIMIT = 56 * 1024 * 1024
MIXER_VMEM_LIMIT = 60 * 1024 * 1024


def _sigmoid(v):
    return 1.0 / (1.0 + jnp.exp(-v))


def _rms(v, g):
    return v * lax.rsqrt(jnp.mean(v * v, axis=-1, keepdims=True) + EPS) * g


def _dot(a, b):
    return jnp.dot(a, b, preferred_element_type=F32)


def _conv_steps(buf, w_ref, width, halo, rows, emit):
    steps = []
    for j in range(buf.shape[0]):
        for base in range(0, rows, CONV_ROWS):
            def step(j=j, base=base):
                lanes = slice(j * LANES, (j + 1) * LANES)
                acc = None
                for k in range(width):
                    start = base + halo - (width - 1) + k
                    term = w_ref[k:k + 1, lanes] * buf[j, start:start + CONV_ROWS, :]
                    acc = term if acc is None else acc + term
                emit(slice(base, base + CONV_ROWS), lanes, acc)
            steps.append((width * CONV_ROWS // 8, step))
    return steps


def _packed_conv_steps(packed, w_ref, width, halo, rows, emit):
    group = 2 * SUBLANES
    steps = []
    for j in range(packed.shape[1]):
        for base in range(0, rows, CONV_ROWS):
            def step(j=j, base=base):
                accs = [None] * (CONV_ROWS // group)
                for k in range(width):
                    shift = k % 2
                    tap = w_ref[k, j]
                    for g in range(len(accs)):
                        word = (base + g * group + halo - (width - 1) + k - shift) // 2
                        vals = pltpu.bitcast(packed[shift, j, word:word + SUBLANES, :], BF16)
                        accs[g] = vals * tap if accs[g] is None else accs[g] + vals * tap
                for g, acc in enumerate(accs):
                    lo = base + g * group
                    emit(slice(lo, lo + group), slice(j * LANES, (j + 1) * LANES), acc.astype(F32))
            steps.append((width * CONV_ROWS // 16, step))
    return steps


def _interleave(mxu_steps, vpu_steps):
    total_m = sum(c for c, _ in mxu_steps)
    total_v = sum(c for c, _ in vpu_steps)
    done_m = 0
    done_v = 0
    pending = list(vpu_steps)
    for cost, step in mxu_steps:
        step()
        done_m += cost
        while pending and done_v * total_m < done_m * total_v:
            c, vstep = pending.pop(0)
            vstep()
            done_v += c
    for _, vstep in pending:
        vstep()


def _mixer_kernel(x_ref, g1_ref, win_ref, caw_ref, woa_ref, cbw_ref, cbb_ref, lng_ref, lnb_ref,
                  wob_ref, bg_ref, wo_ref, g2_ref, wr_ref, br_ref,
                  x1_ref, lrow_ref, gw_ref, np_ref, hs_ref,
                  abuf, bbuf, bpk, cbuf, pbuf, zbuf, swbuf, *, width_a, width_b, d_model, wa, wb):
    ts = x_ref.shape[0]

    @pl.when(pl.program_id(1) == 0)
    def _():
        abuf[:, 0:HALO_A, :] = jnp.zeros((abuf.shape[0], HALO_A, LANES), F32)
        bbuf[:, 0:HALO_B, :] = jnp.zeros((bbuf.shape[0], HALO_B, LANES), F32)
        bbuf[:, HALO_B + ts:, :] = jnp.zeros((bbuf.shape[0], SUBLANES, LANES), F32)

    h = _rms(x_ref[...], g1_ref[...]).astype(BF16)
    col_b, col_c, col_x = 0, wa, 2 * wa
    col_v, col_g, col_gate = 3 * wa, 3 * wa + wb, 3 * wa + 2 * wb

    def proj(col, q):
        return _dot(h, win_ref[:, col + q * MXU_COLS:col + (q + 1) * MXU_COLS])

    for q in range(wb // MXU_COLS):
        u = proj(col_v, q) * _sigmoid(proj(col_g, q))
        for jj in range(MXU_COLS // LANES):
            bbuf[q * (MXU_COLS // LANES) + jj, HALO_B:HALO_B + ts, :] = u[:, jj * LANES:(jj + 1) * LANES]

    raw_cols = ((col_c, wa), (col_x, wa), (col_b, wa), (col_gate, 2 * d_model))
    pb_c, pb_x, pb_b, pb_g = 0, wa, 2 * wa, 3 * wa
    mxu_steps = []
    off = 0
    for col, width in raw_cols:
        for q in range(width // MXU_COLS):
            def raw_step(col=col, q=q, off=off):
                lo = off + q * MXU_COLS
                pbuf[:, lo:lo + MXU_COLS] = proj(col, q)
            mxu_steps.append((1, raw_step))
        off += width

    def emit_b(rows, lanes, acc):
        cbuf[rows, lanes] = acc

    for j in range(wb // LANES):
        for shift in range(2):
            rows_f32 = bbuf[j, shift:shift + HALO_B + ts, :]
            bpk[shift, j] = pltpu.bitcast(rows_f32.astype(BF16), jnp.uint32)

    _interleave(mxu_steps, _packed_conv_steps(bpk, cbw_ref, width_b, HALO_B, ts, emit_b))
    bbuf[:, 0:HALO_B, :] = bbuf[:, ts:ts + HALO_B, :]

    for j in range(wa // LANES):
        abuf[j, HALO_A:HALO_A + ts, :] = (pbuf[:, pb_c + j * LANES:pb_c + (j + 1) * LANES]
                                           * pbuf[:, pb_x + j * LANES:pb_x + (j + 1) * LANES])

    def emit_a(rows, lanes, acc):
        zbuf[rows, lanes] = (pbuf[rows, pb_b + lanes.start:pb_b + lanes.stop] * acc).astype(BF16)

    for _, step in _conv_steps(abuf, caw_ref, width_a, HALO_A, ts, emit_a):
        step()
    abuf[:, 0:HALO_A, :] = abuf[:, ts:ts + HALO_A, :]

    for r in range(0, ts, TAIL_ROWS):
        rows = slice(r, r + TAIL_ROWS)
        c = cbuf[rows, :] + cbb_ref[...]
        mu = jnp.mean(c, axis=-1, keepdims=True)
        cc = c - mu
        var = jnp.mean(cc * cc, axis=-1, keepdims=True)
        ln = cc * lax.rsqrt(var + EPS) * lng_ref[...] + lnb_ref[...]
        swbuf[rows, :] = (ln * _sigmoid(ln)).astype(BF16)
        gcols = slice(pb_g, pb_g + 2 * d_model)
        pbuf[rows, gcols] = _sigmoid(pbuf[rows, gcols] + bg_ref[...])
    mix = (pbuf[:, pb_g:pb_g + d_model] * _dot(zbuf[...], woa_ref[...])
           + pbuf[:, pb_g + d_model:pb_g + 2 * d_model] * _dot(swbuf[...], wob_ref[...]))

    x1 = x_ref[...] + _dot(mix.astype(BF16), wo_ref[...])
    x1_ref[...] = x1

    h2 = _rms(x1, g2_ref[...])
    logits = lax.dot_general(wr_ref[...], h2, (((1,), (1,)), ((), ())),
                             preferred_element_type=F32) + br_ref[:, 0:1]
    gl = logits[0:N_GROUPS, :]
    gmax = jnp.max(gl, axis=0, keepdims=True)
    giota = lax.broadcasted_iota(jnp.int32, gl.shape, 0)
    grp = jnp.min(jnp.where(gl == gmax, giota, N_GROUPS), axis=0, keepdims=True)
    p_grp = 1.0 / jnp.sum(jnp.exp(gl - gmax), axis=0, keepdims=True)

    el = logits[8:8 + N_EXPERTS, :]
    eiota = lax.broadcasted_iota(jnp.int32, el.shape, 0)
    neg = jnp.float32(-jnp.inf)
    m0 = jnp.where(lax.div(eiota, EXPERTS_PER_GROUP) == grp, el, neg)
    v1 = jnp.max(m0, axis=0, keepdims=True)
    i1 = jnp.min(jnp.where(m0 == v1, eiota, N_EXPERTS), axis=0, keepdims=True)
    m1 = jnp.where(eiota == i1, neg, m0)
    v2 = jnp.max(m1, axis=0, keepdims=True)
    i2 = jnp.min(jnp.where(m1 == v2, eiota, N_EXPERTS), axis=0, keepdims=True)
    e2 = jnp.exp(v2 - v1)
    w1 = p_grp / (1.0 + e2)
    w2 = p_grp * e2 / (1.0 + e2)

    oh1 = eiota == i1
    oh2 = eiota == i2
    used = jnp.where(oh1, 1.0, jnp.where(oh2, 1.0, 0.0))
    rio = lax.broadcasted_iota(jnp.int32, (ts, ts), 0)
    cio = lax.broadcasted_iota(jnp.int32, (ts, ts), 1)
    before = jnp.where(rio < cio, 1.0, 0.0).astype(BF16)
    in_strip = _dot(used.astype(BF16), before)
    pieces = jnp.floor((jnp.sum(used, axis=1, keepdims=True) + (PIECE - 1.0)) * (1.0 / PIECE))
    er = lax.broadcasted_iota(jnp.int32, (N_EXPERTS, N_EXPERTS), 0)
    ec = lax.broadcasted_iota(jnp.int32, (N_EXPERTS, N_EXPERTS), 1)
    earlier = jnp.where(ec < er, 1.0, 0.0)
    strip_start = _dot(earlier, jnp.broadcast_to(pieces * PIECE, (N_EXPERTS, LANES)))[:, 0:1]
    row = strip_start + in_strip
    r1 = jnp.sum(jnp.where(oh1, row, 0.0), axis=0, keepdims=True)
    r2 = jnp.sum(jnp.where(oh2, row, 0.0), axis=0, keepdims=True)

    lrow = jnp.concatenate([r1, r2], axis=0).astype(jnp.int32)
    lrow_ref[...] = lrow
    gw_ref[...] = jnp.concatenate([w1, w2], axis=0)
    np_ref[...] = jnp.broadcast_to(pieces, np_ref.shape).astype(jnp.int32)

    riota = lax.broadcasted_iota(jnp.int32, (hs_ref.shape[0], ts), 0)
    pick = jnp.where(riota == lrow[0:1, :], 1.0, jnp.where(riota == lrow[1:2, :], 1.0, 0.0))
    hs_ref[...] = _dot(pick.astype(BF16), h2.astype(BF16))


def _const_spec(shape):
    return pl.BlockSpec(shape, lambda *_: (0,) * len(shape), pipeline_mode=pl.Buffered(1))


def _mixer(x, g1, w_in, conv_a_w, w_out_a, conv_b_w, conv_b_b, ln_g, ln_b, w_out_b, b_gate, w_o,
           g2, w_r, b_r):
    bsz, seq, d = x.shape
    wa = w_out_a.shape[0]
    wb = w_out_b.shape[0]
    ts = SEQ_TILE
    n_s = seq // ts
    t = bsz * seq
    taps_b = conv_b_w.astype(BF16).reshape(conv_b_w.shape[0], wb // LANES, 1, LANES)
    taps_b = jnp.broadcast_to(taps_b, taps_b.shape[:2] + (2 * SUBLANES, LANES))
    consts = (g1, w_in, conv_a_w, w_out_a, taps_b, conv_b_b, ln_g, ln_b, w_out_b, b_gate, w_o,
              g2, w_r, b_r)
    tok_spec = pl.BlockSpec((TOP_K, ts), lambda b, s: (0, b * n_s + s))
    return pl.pallas_call(
        functools.partial(_mixer_kernel, width_a=conv_a_w.shape[0], width_b=conv_b_w.shape[0],
                          d_model=d, wa=wa, wb=wb),
        grid=(bsz, n_s),
        in_specs=[pl.BlockSpec((None, ts, d), lambda b, s: (b, s, 0))]
                 + [_const_spec(c.shape) for c in consts],
        out_specs=[pl.BlockSpec((None, ts, d), lambda b, s: (b, s, 0)),
                   tok_spec, tok_spec,
                   pl.BlockSpec((None, N_EXPERTS, LANES), lambda b, s: (b * n_s + s, 0, 0)),
                   pl.BlockSpec((SORTED_ROWS, d), lambda b, s: (b * n_s + s, 0))],
        out_shape=[jax.ShapeDtypeStruct((bsz, seq, d), F32),
                   jax.ShapeDtypeStruct((TOP_K, t), jnp.int32),
                   jax.ShapeDtypeStruct((TOP_K, t), F32),
                   jax.ShapeDtypeStruct((t // ts, N_EXPERTS, LANES), jnp.int32),
                   jax.ShapeDtypeStruct((t // ts * SORTED_ROWS, d), F32)],
        scratch_shapes=[pltpu.VMEM((wa // LANES, HALO_A + ts, LANES), F32),
                        pltpu.VMEM((wb // LANES, HALO_B + ts + SUBLANES, LANES), F32),
                        pltpu.VMEM((2, wb // LANES, (HALO_B + ts) // 2, LANES), jnp.uint32),
                        pltpu.VMEM((ts, wb), F32),
                        pltpu.VMEM((ts, 3 * wa + 2 * d), F32),
                        pltpu.VMEM((ts, wa), BF16),
                        pltpu.VMEM((ts, wb), BF16)],
        compiler_params=pltpu.CompilerParams(
            dimension_semantics=("arbitrary", "arbitrary"), vmem_limit_bytes=MIXER_VMEM_LIMIT),
        name="mixer",
    )(x, *consts)


def _zero_unwritten(used_ref, ys_ref, zero, sem):
    n_tiles = used_ref.shape[0]
    rb = zero.shape[0]

    def fill(row, n_rows):
        return pltpu.make_async_copy(zero.at[pl.ds(0, n_rows), :],
                                     ys_ref.at[pl.ds(row, n_rows), :], sem)

    dump = [fill(r, rb) for r in range(n_tiles * SORTED_ROWS, ys_ref.shape[0], rb)]

    def tails(wait):
        def body(i, carry):
            used = used_ref[i]
            tail = SORTED_ROWS - used
            row = i * SORTED_ROWS + used
            size = SORTED_ROWS - TOP_K * SEQ_TILE
            while size >= PIECE:
                @pl.when((tail & size) != 0)
                def _(row=row, size=size):
                    c = fill(pl.multiple_of(row, PIECE), size)
                    c.wait() if wait else c.start()
                row = row + (tail & size)
                size //= 2
            return carry
        lax.fori_loop(0, n_tiles, body, 0)

    zero[...] = jnp.zeros(zero.shape, F32)
    for c in dump:
        c.start()
    tails(False)
    for c in dump:
        c.wait()
    tails(True)


def _expert_kernel(bexp_ref, nused_ref, src_ref, dst_ref, used_ref, hs_ref, wg_ref, wu_ref, wd_ref,
                   ys_ref, xbuf0, xbuf1, ybuf0, ybuf1, wg_bf, wu_bf, wd_bf, sem_in, sem_out):
    b = pl.program_id(0)
    n_used = nused_ref[0]
    xbufs = (xbuf0, xbuf1)
    ybufs = (ybuf0, ybuf1)
    rb = xbuf0.shape[0]

    @pl.when(b == 0)
    def _():
        _zero_unwritten(used_ref, ys_ref, ybuf1, sem_out.at[1])

    def piece_copies(table_ref, blk, hbm_ref, buf, sem, inbound):
        copies = []
        for j in range(PIECES_PER_BLOCK):
            row = pl.multiple_of(table_ref[blk * PIECES_PER_BLOCK + j] * PIECE, PIECE)
            hbm = hbm_ref.at[pl.ds(row, PIECE), :]
            vmem = buf.at[pl.ds(j * PIECE, PIECE), :]
            copies.append(pltpu.make_async_copy(hbm, vmem, sem) if inbound
                          else pltpu.make_async_copy(vmem, hbm, sem))
        return copies

    def start_in(blk, slot):
        for c in piece_copies(src_ref, blk, hs_ref, xbufs[slot], sem_in.at[slot], True):
            c.start()

    def start_out(blk, slot):
        for c in piece_copies(dst_ref, blk, ys_ref, ybufs[slot], sem_out.at[slot], False):
            c.start()

    def wait_in(slot):
        pltpu.make_async_copy(hs_ref.at[pl.ds(0, rb), :], xbufs[slot], sem_in.at[slot]).wait()

    def wait_out(slot):
        pltpu.make_async_copy(ybufs[slot], ys_ref.at[pl.ds(0, rb), :], sem_out.at[slot]).wait()

    def block(slot):
        other = 1 - slot
        prev = jnp.maximum(b - 1, 0)

        @pl.when((b == 0) | (bexp_ref[prev] != bexp_ref[b]))
        def _():
            wg_bf[...] = wg_ref[...].astype(BF16)
            wu_bf[...] = wu_ref[...].astype(BF16)
            wd_bf[...] = wd_ref[...].astype(BF16)

        wait_in(slot)

        @pl.when(b >= 2)
        def _():
            wait_out(slot)

        start_in(jnp.minimum(b + 1, n_used - 1), other)
        xb = xbufs[slot][...].astype(BF16)
        a = _dot(xb, wg_bf[...])
        hid = (a * _sigmoid(a) * _dot(xb, wu_bf[...])).astype(BF16)
        ybufs[slot][...] = _dot(hid, wd_bf[...])
        start_out(b, slot)

        @pl.when(b == n_used - 1)
        def _():
            wait_in(other)

            @pl.when(b >= 1)
            def _():
                wait_out(other)
            wait_out(slot)

    @pl.when(b == 0)
    def _():
        start_in(0, 0)

    for slot in range(2):
        @pl.when((b < n_used) & (lax.rem(b, 2) == slot))
        def _(slot=slot):
            block(slot)


def _experts(hs, w_gate, w_up, w_down, bexp, nused, src, dst, used_rows, n_out_rows):
    _, d = hs.shape
    _, _, de = w_gate.shape
    rb = EXPERT_TILE
    assert (n_out_rows - hs.shape[0]) % rb == 0
    wmap = lambda b, bexp, nused, src, dst, used: (bexp[b], 0, 0)
    grid_spec = pltpu.PrefetchScalarGridSpec(
        num_scalar_prefetch=5,
        grid=(bexp.shape[0],),
        in_specs=[pl.BlockSpec(memory_space=pl.ANY),
                  pl.BlockSpec((None, d, de), wmap),
                  pl.BlockSpec((None, d, de), wmap),
                  pl.BlockSpec((None, de, d), wmap)],
        out_specs=pl.BlockSpec(memory_space=pl.ANY),
        scratch_shapes=[pltpu.VMEM((rb, d), F32), pltpu.VMEM((rb, d), F32),
                        pltpu.VMEM((rb, d), F32), pltpu.VMEM((rb, d), F32),
                        pltpu.VMEM((d, de), BF16), pltpu.VMEM((d, de), BF16),
                        pltpu.VMEM((de, d), BF16),
                        pltpu.SemaphoreType.DMA((2,)), pltpu.SemaphoreType.DMA((2,))],
    )
    return pl.pallas_call(
        _expert_kernel,
        grid_spec=grid_spec,
        out_shape=jax.ShapeDtypeStruct((n_out_rows, d), F32),
        compiler_params=pltpu.CompilerParams(dimension_semantics=("arbitrary",),
                                             vmem_limit_bytes=VMEM_LIMIT),
        name="experts",
    )(bexp, nused, src, dst, used_rows, hs, w_gate, w_up, w_down)


def _combine_kernel(x1_ref, p_ref, lrow_ref, gw_ref, g3_ref, wpg_ref, wpp_ref, gf_ref, ys_ref,
                    out_ref):
    ts = x1_ref.shape[0]
    ciota = lax.broadcasted_iota(jnp.int32, (ts, ys_ref.shape[0]), 1)
    pick = jnp.where(ciota == lrow_ref[:, 0:1], gw_ref[:, 0:1],
                     jnp.where(ciota == lrow_ref[:, 1:2], gw_ref[:, 1:2], 0.0))
    x2 = x1_ref[...] + _dot(pick.astype(BF16), ys_ref[...].astype(BF16))
    hp = _rms(x2, g3_ref[...]).astype(BF16)
    proj = _dot(p_ref[...].astype(BF16), wpp_ref[...])
    x3 = x2 + _sigmoid(_dot(hp, wpg_ref[...])) * proj
    out_ref[...] = _rms(x3, gf_ref[...])


def _combine(x1, p, lrow_t, gw_t, ys, g3, w_pg, w_pp, gf):
    t, d = x1.shape
    pd = p.shape[1]
    ts = SEQ_TILE
    tile = lambda i: (i, 0)
    const = lambda i: (0, 0)
    return pl.pallas_call(
        _combine_kernel,
        grid=(t // ts,),
        in_specs=[pl.BlockSpec((ts, d), tile),
                  pl.BlockSpec((ts, pd), tile),
                  pl.BlockSpec((ts, TOP_K), tile),
                  pl.BlockSpec((ts, TOP_K), tile),
                  pl.BlockSpec((1, d), const),
                  pl.BlockSpec((d, d), const),
                  pl.BlockSpec((pd, d), const),
                  pl.BlockSpec((1, d), const),
                  pl.BlockSpec((SORTED_ROWS, d), tile)],
        out_specs=pl.BlockSpec((ts, d), tile),
        out_shape=jax.ShapeDtypeStruct((t, d), F32),
        compiler_params=pltpu.CompilerParams(dimension_semantics=("arbitrary",),
                                             vmem_limit_bytes=VMEM_LIMIT),
        name="combine",
    )(x1, p, lrow_t, gw_t, g3, w_pg, w_pp, gf, ys)


def _piece_plan(n_pieces):
    n_tiles = n_pieces.shape[0]
    ppb = PIECES_PER_BLOCK
    tile_pieces = SORTED_ROWS // PIECE
    i32 = jnp.int32
    excl = lambda v, axis: jnp.cumsum(v, axis=axis, dtype=i32) - v
    base = (jnp.arange(n_tiles, dtype=i32) * tile_pieces)[:, None] + excl(n_pieces, 1)
    per_expert = n_pieces.T
    blocks = (jnp.sum(per_expert, axis=1, dtype=i32) + ppb - 1) // ppb
    first_block = excl(blocks, 0)
    n_used = jnp.sum(blocks, dtype=i32)

    max_blocks = (TOP_K * n_tiles * SEQ_TILE // PIECE + n_tiles * N_EXPERTS) // ppb + N_EXPERTS
    block_ids = jnp.arange(max_blocks, dtype=i32)
    blk = jnp.minimum(block_ids, n_used - 1)
    bexp = jnp.sum((first_block[None, :] <= blk[:, None]).astype(i32), axis=1) - 1
    strip_lo = jnp.take(excl(per_expert, 1), bexp, axis=0)[:, None, :]
    strip_len = jnp.take(per_expert, bexp, axis=0)[:, None, :]
    strip_base = jnp.take(base.T, bexp, axis=0)[:, None, :]
    local = ((block_ids - jnp.take(first_block, bexp)) * ppb)[:, None] + jnp.arange(ppb, dtype=i32)
    local = local[:, :, None]
    inside = (local >= strip_lo) & (local < strip_lo + strip_len)
    src = jnp.sum(jnp.where(inside, strip_base + local - strip_lo, 0), axis=2, dtype=i32).reshape(-1)
    real = jnp.any(inside, axis=2).reshape(-1)
    n_dump = N_EXPERTS * ppb
    pad_rank = jnp.minimum(jnp.cumsum(jnp.logical_not(real), dtype=i32) - 1, n_dump - 1)
    dst = jnp.where(real, src, n_tiles * tile_pieces + pad_rank)
    n_out_rows = (n_tiles * tile_pieces + n_dump) * PIECE
    return bexp, n_used.reshape(1), src, dst, n_out_rows


def _layer(x, p, mix_norm_g, w_in, conv_a_w, w_out_a, conv_b_w, conv_b_b, ln_b_g, ln_b_b, w_out_b,
           b_gate, w_o, ffn_norm_g, w_rg, b_rg, w_re, b_re, w_eg, w_eu, w_ed, ple_norm_g, w_pg, w_pp,
           out_norm_g):
    bsz, seq, d = x.shape
    t = bsz * seq
    row = lambda v: v.reshape(1, -1).astype(F32)
    w_r = jnp.zeros((ROUTER_ROWS, d), F32).at[0:N_GROUPS].set(w_rg.T).at[8:].set(w_re.T)
    b_r = jnp.zeros((ROUTER_ROWS,), F32).at[0:N_GROUPS].set(b_rg).at[8:].set(b_re)
    b_r = jnp.broadcast_to(b_r[:, None], (ROUTER_ROWS, LANES))

    x1, lrow, gw, n_pieces, hs = _mixer(
        x, row(mix_norm_g), w_in.astype(BF16), conv_a_w, w_out_a.astype(BF16), conv_b_w,
        row(conv_b_b), row(ln_b_g), row(ln_b_b), w_out_b.astype(BF16), row(b_gate),
        w_o.astype(BF16), row(ffn_norm_g), w_r, b_r)
    x1 = x1.reshape(t, d)

    n_pieces = n_pieces[:, :, 0]
    bexp, n_used, src, dst, n_out_rows = _piece_plan(n_pieces)
    used_rows = jnp.sum(n_pieces, axis=1, dtype=jnp.int32) * PIECE
    ys = _experts(hs, w_eg, w_eu, w_ed, bexp, n_used, src, dst, used_rows, n_out_rows)
    out = _combine(x1, p.reshape(t, -1), lrow.T, gw.T, ys, row(ple_norm_g),
                   w_pg.astype(BF16), w_pp.astype(BF16), row(out_norm_g))
    return out.reshape(bsz, seq, d)


def kernel(x, p, mix_norm_g, w_in, conv_a_w, w_out_a, conv_b_w, conv_b_b, ln_b_g, ln_b_b, w_out_b, b_gate, w_o, ffn_norm_g, w_router_group, b_router_group, w_router_expert, b_router_expert, w_exp_gate, w_exp_up, w_exp_down, ple_norm_g, w_ple_gate, w_ple_proj, final_norm_g):
    assert w_in.shape[0] == 1, "the combine kernel fuses the final RMSNorm: single layer only"
    return _layer(x, p[0], mix_norm_g[0], w_in[0], conv_a_w[0], w_out_a[0], conv_b_w[0],
                  conv_b_b[0], ln_b_g[0], ln_b_b[0], w_out_b[0], b_gate[0], w_o[0], ffn_norm_g[0],
                  w_router_group[0], b_router_group[0], w_router_expert[0], b_router_expert[0],
                  w_exp_gate[0], w_exp_up[0], w_exp_down[0], ple_norm_g[0], w_ple_gate[0],
                  w_ple_proj[0], final_norm_g)
```

```python
import functools

import jax
import jax.numpy as jnp
from jax import lax
from jax.experimental import pallas as pl
from jax.experimental.pallas import tpu as pltpu

F32 = jnp.float32
BF16 = jnp.bfloat16
EPS = 1e-6

LANES = 128
SUBLANES = 8
N_GROUPS = 4
EXPERTS_PER_GROUP = 8
N_EXPERTS = N_GROUPS * EXPERTS_PER_GROUP
TOP_K = 2
ROUTER_ROWS = 8 + N_EXPERTS

SEQ_TILE = 512
CONV_ROWS = 128
MXU_COLS = 256
HALO_A = 8
HALO_B = 32
TAIL_ROWS = 128
PIECE = SUBLANES
EXPERT_TILE = 512
PIECES_PER_BLOCK = EXPERT_TILE // PIECE
SORTED_ROWS = TOP_K * SEQ_TILE + N_EXPERTS * PIECE
VMEM_LIMIT = 56 * 1024 * 1024
MIXER_VMEM_LIMIT = 60 * 1024 * 1024


def _sigmoid(v):
    return 1.0 / (1.0 + jnp.exp(-v))


def _rms(v, g):
    return v * lax.rsqrt(jnp.mean(v * v, axis=-1, keepdims=True) + EPS) * g


def _dot(a, b):
    return jnp.dot(a, b, preferred_element_type=F32)


def _conv_steps(buf, w_ref, width, halo, rows, emit):
    steps = []
    for j in range(buf.shape[0]):
        for base in range(0, rows, CONV_ROWS):
            def step(j=j, base=base):
                lanes = slice(j * LANES, (j + 1) * LANES)
                acc = None
                for k in range(width):
                    start = base + halo - (width - 1) + k
                    term = w_ref[k:k + 1, lanes] * buf[j, start:start + CONV_ROWS, :]
                    acc = term if acc is None else acc + term
                emit(slice(base, base + CONV_ROWS), lanes, acc)
            steps.append((width * CONV_ROWS // 8, step))
    return steps


def _packed_conv_steps(packed, w_ref, width, halo, rows, emit):
    group = 2 * SUBLANES
    steps = []
    for j in range(packed.shape[1]):
        for base in range(0, rows, CONV_ROWS):
            def step(j=j, base=base):
                accs = [None] * (CONV_ROWS // group)
                for k in range(width):
                    shift = k % 2
                    tap = w_ref[k, j]
                    for g in range(len(accs)):
                        word = (base + g * group + halo - (width - 1) + k - shift) // 2
                        vals = pltpu.bitcast(packed[shift, j, word:word + SUBLANES, :], BF16)
                        accs[g] = vals * tap if accs[g] is None else accs[g] + vals * tap
                for g, acc in enumerate(accs):
                    lo = base + g * group
                    emit(slice(lo, lo + group), slice(j * LANES, (j + 1) * LANES), acc.astype(F32))
            steps.append((width * CONV_ROWS // 16, step))
    return steps


def _interleave(mxu_steps, vpu_steps):
    total_m = sum(c for c, _ in mxu_steps)
    total_v = sum(c for c, _ in vpu_steps)
    done_m = 0
    done_v = 0
    pending = list(vpu_steps)
    for cost, step in mxu_steps:
        step()
        done_m += cost
        while pending and done_v * total_m < done_m * total_v:
            c, vstep = pending.pop(0)
            vstep()
            done_v += c
    for _, vstep in pending:
        vstep()


def _mixer_kernel(x_ref, g1_ref, win_ref, caw_ref, woa_ref, cbw_ref, cbb_ref, lng_ref, lnb_ref,
                  wob_ref, bg_ref, wo_ref, g2_ref, wr_ref, br_ref,
                  x1_ref, lrow_ref, gw_ref, np_ref, hs_ref,
                  abuf, bbuf, bpk, cbuf, pbuf, zbuf, swbuf, *, width_a, width_b, d_model, wa, wb):
    ts = x_ref.shape[0]

    @pl.when(pl.program_id(1) == 0)
    def _():
        abuf[:, 0:HALO_A, :] = jnp.zeros((abuf.shape[0], HALO_A, LANES), F32)
        bbuf[:, 0:HALO_B, :] = jnp.zeros((bbuf.shape[0], HALO_B, LANES), F32)
        bbuf[:, HALO_B + ts:, :] = jnp.zeros((bbuf.shape[0], SUBLANES, LANES), F32)

    h = _rms(x_ref[...], g1_ref[...]).astype(BF16)
    col_b, col_c, col_x = 0, wa, 2 * wa
    col_v, col_g, col_gate = 3 * wa, 3 * wa + wb, 3 * wa + 2 * wb

    def proj(col, q):
        return _dot(h, win_ref[:, col + q * MXU_COLS:col + (q + 1) * MXU_COLS])

    for q in range(wb // MXU_COLS):
        u = proj(col_v, q) * _sigmoid(proj(col_g, q))
        for jj in range(MXU_COLS // LANES):
            bbuf[q * (MXU_COLS // LANES) + jj, HALO_B:HALO_B + ts, :] = u[:, jj * LANES:(jj + 1) * LANES]

    raw_cols = ((col_c, wa), (col_x, wa), (col_b, wa), (col_gate, 2 * d_model))
    pb_c, pb_x, pb_b, pb_g = 0, wa, 2 * wa, 3 * wa
    mxu_steps = []
    off = 0
    for col, width in raw_cols:
        for q in range(width // MXU_COLS):
            def raw_step(col=col, q=q, off=off):
                lo = off + q * MXU_COLS
                pbuf[:, lo:lo + MXU_COLS] = proj(col, q)
            mxu_steps.append((1, raw_step))
        off += width

    def emit_b(rows, lanes, acc):
        cbuf[rows, lanes] = acc

    for j in range(wb // LANES):
        for shift in range(2):
            rows_f32 = bbuf[j, shift:shift + HALO_B + ts, :]
            bpk[shift, j] = pltpu.bitcast(rows_f32.astype(BF16), jnp.uint32)

    _interleave(mxu_steps, _packed_conv_steps(bpk, cbw_ref, width_b, HALO_B, ts, emit_b))
    bbuf[:, 0:HALO_B, :] = bbuf[:, ts:ts + HALO_B, :]

    for j in range(wa // LANES):
        abuf[j, HALO_A:HALO_A + ts, :] = (pbuf[:, pb_c + j * LANES:pb_c + (j + 1) * LANES]
                                           * pbuf[:, pb_x + j * LANES:pb_x + (j + 1) * LANES])

    def emit_a(rows, lanes, acc):
        zbuf[rows, lanes] = (pbuf[rows, pb_b + lanes.start:pb_b + lanes.stop] * acc).astype(BF16)

    for _, step in _conv_steps(abuf, caw_ref, width_a, HALO_A, ts, emit_a):
        step()
    abuf[:, 0:HALO_A, :] = abuf[:, ts:ts + HALO_A, :]

    for r in range(0, ts, TAIL_ROWS):
        rows = slice(r, r + TAIL_ROWS)
        c = cbuf[rows, :] + cbb_ref[...]
        mu = jnp.mean(c, axis=-1, keepdims=True)
        cc = c - mu
        var = jnp.mean(cc * cc, axis=-1, keepdims=True)
        ln = cc * lax.rsqrt(var + EPS) * lng_ref[...] + lnb_ref[...]
        swbuf[rows, :] = (ln * _sigmoid(ln)).astype(BF16)
        gcols = slice(pb_g, pb_g + 2 * d_model)
        pbuf[rows, gcols] = _sigmoid(pbuf[rows, gcols] + bg_ref[...])
    mix = (pbuf[:, pb_g:pb_g + d_model] * _dot(zbuf[...], woa_ref[...])
           + pbuf[:, pb_g + d_model:pb_g + 2 * d_model] * _dot(swbuf[...], wob_ref[...]))

    x1 = x_ref[...] + _dot(mix.astype(BF16), wo_ref[...])
    x1_ref[...] = x1

    h2 = _rms(x1, g2_ref[...])
    logits = lax.dot_general(wr_ref[...], h2, (((1,), (1,)), ((), ())),
                             preferred_element_type=F32) + br_ref[:, 0:1]
    gl = logits[0:N_GROUPS, :]
    gmax = jnp.max(gl, axis=0, keepdims=True)
    giota = lax.broadcasted_iota(jnp.int32, gl.shape, 0)
    grp = jnp.min(jnp.where(gl == gmax, giota, N_GROUPS), axis=0, keepdims=True)
    p_grp = 1.0 / jnp.sum(jnp.exp(gl - gmax), axis=0, keepdims=True)

    el = logits[8:8 + N_EXPERTS, :]
    eiota = lax.broadcasted_iota(jnp.int32, el.shape, 0)
    neg = jnp.float32(-jnp.inf)
    m0 = jnp.where(lax.div(eiota, EXPERTS_PER_GROUP) == grp, el, neg)
    v1 = jnp.max(m0, axis=0, keepdims=True)
    i1 = jnp.min(jnp.where(m0 == v1, eiota, N_EXPERTS), axis=0, keepdims=True)
    m1 = jnp.where(eiota == i1, neg, m0)
    v2 = jnp.max(m1, axis=0, keepdims=True)
    i2 = jnp.min(jnp.where(m1 == v2, eiota, N_EXPERTS), axis=0, keepdims=True)
    e2 = jnp.exp(v2 - v1)
    w1 = p_grp / (1.0 + e2)
    w2 = p_grp * e2 / (1.0 + e2)

    oh1 = eiota == i1
    oh2 = eiota == i2
    used = jnp.where(oh1, 1.0, jnp.where(oh2, 1.0, 0.0))
    rio = lax.broadcasted_iota(jnp.int32, (ts, ts), 0)
    cio = lax.broadcasted_iota(jnp.int32, (ts, ts), 1)
    before = jnp.where(rio < cio, 1.0, 0.0).astype(BF16)
    in_strip = _dot(used.astype(BF16), before)
    pieces = jnp.floor((jnp.sum(used, axis=1, keepdims=True) + (PIECE - 1.0)) * (1.0 / PIECE))
    er = lax.broadcasted_iota(jnp.int32, (N_EXPERTS, N_EXPERTS), 0)
    ec = lax.broadcasted_iota(jnp.int32, (N_EXPERTS, N_EXPERTS), 1)
    earlier = jnp.where(ec < er, 1.0, 0.0)
    strip_start = _dot(earlier, jnp.broadcast_to(pieces * PIECE, (N_EXPERTS, LANES)))[:, 0:1]
    row = strip_start + in_strip
    r1 = jnp.sum(jnp.where(oh1, row, 0.0), axis=0, keepdims=True)
    r2 = jnp.sum(jnp.where(oh2, row, 0.0), axis=0, keepdims=True)

    lrow = jnp.concatenate([r1, r2], axis=0).astype(jnp.int32)
    lrow_ref[...] = lrow
    gw_ref[...] = jnp.concatenate([w1, w2], axis=0)
    np_ref[...] = jnp.broadcast_to(pieces, np_ref.shape).astype(jnp.int32)

    riota = lax.broadcasted_iota(jnp.int32, (hs_ref.shape[0], ts), 0)
    pick = jnp.where(riota == lrow[0:1, :], 1.0, jnp.where(riota == lrow[1:2, :], 1.0, 0.0))
    hs_ref[...] = _dot(pick.astype(BF16), h2.astype(BF16))


def _const_spec(shape):
    return pl.BlockSpec(shape, lambda *_: (0,) * len(shape), pipeline_mode=pl.Buffered(1))


def _mixer(x, g1, w_in, conv_a_w, w_out_a, conv_b_w, conv_b_b, ln_g, ln_b, w_out_b, b_gate, w_o,
           g2, w_r, b_r):
    bsz, seq, d = x.shape
    wa = w_out_a.shape[0]
    wb = w_out_b.shape[0]
    ts = SEQ_TILE
    n_s = seq // ts
    t = bsz * seq
    taps_b = conv_b_w.astype(BF16).reshape(conv_b_w.shape[0], wb // LANES, 1, LANES)
    taps_b = jnp.broadcast_to(taps_b, taps_b.shape[:2] + (2 * SUBLANES, LANES))
    consts = (g1, w_in, conv_a_w, w_out_a, taps_b, conv_b_b, ln_g, ln_b, w_out_b, b_gate, w_o,
              g2, w_r, b_r)
    tok_spec = pl.BlockSpec((TOP_K, ts), lambda b, s: (0, b * n_s + s))
    return pl.pallas_call(
        functools.partial(_mixer_kernel, width_a=conv_a_w.shape[0], width_b=conv_b_w.shape[0],
                          d_model=d, wa=wa, wb=wb),
        grid=(bsz, n_s),
        in_specs=[pl.BlockSpec((None, ts, d), lambda b, s: (b, s, 0))]
                 + [_const_spec(c.shape) for c in consts],
        out_specs=[pl.BlockSpec((None, ts, d), lambda b, s: (b, s, 0)),
                   tok_spec, tok_spec,
                   pl.BlockSpec((None, N_EXPERTS, LANES), lambda b, s: (b * n_s + s, 0, 0)),
                   pl.BlockSpec((SORTED_ROWS, d), lambda b, s: (b * n_s + s, 0))],
        out_shape=[jax.ShapeDtypeStruct((bsz, seq, d), F32),
                   jax.ShapeDtypeStruct((TOP_K, t), jnp.int32),
                   jax.ShapeDtypeStruct((TOP_K, t), F32),
                   jax.ShapeDtypeStruct((t // ts, N_EXPERTS, LANES), jnp.int32),
                   jax.ShapeDtypeStruct((t // ts * SORTED_ROWS, d), F32)],
        scratch_shapes=[pltpu.VMEM((wa // LANES, HALO_A + ts, LANES), F32),
                        pltpu.VMEM((wb // LANES, HALO_B + ts + SUBLANES, LANES), F32),
                        pltpu.VMEM((2, wb // LANES, (HALO_B + ts) // 2, LANES), jnp.uint32),
                        pltpu.VMEM((ts, wb), F32),
                        pltpu.VMEM((ts, 3 * wa + 2 * d), F32),
                        pltpu.VMEM((ts, wa), BF16),
                        pltpu.VMEM((ts, wb), BF16)],
        compiler_params=pltpu.CompilerParams(
            dimension_semantics=("arbitrary", "arbitrary"), vmem_limit_bytes=MIXER_VMEM_LIMIT),
        name="mixer",
    )(x, *consts)


def _zero_unwritten(used_ref, ys_ref, zero, sem):
    n_tiles = used_ref.shape[0]
    rb = zero.shape[0]

    def fill(row, n_rows):
        return pltpu.make_async_copy(zero.at[pl.ds(0, n_rows), :],
                                     ys_ref.at[pl.ds(row, n_rows), :], sem)

    dump = [fill(r, rb) for r in range(n_tiles * SORTED_ROWS, ys_ref.shape[0], rb)]

    def tails(wait):
        def body(i, carry):
            used = used_ref[i]
            tail = SORTED_ROWS - used
            row = i * SORTED_ROWS + used
            size = SORTED_ROWS - TOP_K * SEQ_TILE
            while size >= PIECE:
                @pl.when((tail & size) != 0)
                def _(row=row, size=size):
                    c = fill(pl.multiple_of(row, PIECE), size)
                    c.wait() if wait else c.start()
                row = row + (tail & size)
                size //= 2
            return carry
        lax.fori_loop(0, n_tiles, body, 0)

    zero[...] = jnp.zeros(zero.shape, F32)
    for c in dump:
        c.start()
    tails(False)
    for c in dump:
        c.wait()
    tails(True)


def _expert_kernel(bexp_ref, nused_ref, src_ref, dst_ref, used_ref, hs_ref, wg_ref, wu_ref, wd_ref,
                   ys_ref, xbuf, ybuf, wg_bf, wu_bf, wd_bf, sem_in, sem_out):
    b = pl.program_id(0)
    n_used = nused_ref[0]
    rb = xbuf.shape[1]

    @pl.when(b == 0)
    def _():
        _zero_unwritten(used_ref, ys_ref, ybuf.at[1], sem_out.at[1])

    def piece_copies(table_ref, blk, hbm_ref, buf, slot, sem, inbound):
        copies = []
        for j in range(PIECES_PER_BLOCK):
            row = pl.multiple_of(table_ref[blk * PIECES_PER_BLOCK + j] * PIECE, PIECE)
            hbm = hbm_ref.at[pl.ds(row, PIECE), :]
            vmem = buf.at[slot, pl.ds(j * PIECE, PIECE), :]
            copies.append(pltpu.make_async_copy(hbm, vmem, sem.at[slot]) if inbound
                          else pltpu.make_async_copy(vmem, hbm, sem.at[slot]))
        return copies

    def start_in(blk, slot):
        for c in piece_copies(src_ref, blk, hs_ref, xbuf, slot, sem_in, True):
            c.start()

    def start_out(blk, slot):
        for c in piece_copies(dst_ref, blk, ys_ref, ybuf, slot, sem_out, False):
            c.start()

    def wait_in(slot):
        pltpu.make_async_copy(hs_ref.at[pl.ds(0, rb), :], xbuf.at[slot], sem_in.at[slot]).wait()

    def wait_out(slot):
        pltpu.make_async_copy(ybuf.at[slot], ys_ref.at[pl.ds(0, rb), :], sem_out.at[slot]).wait()

    @pl.when(b < n_used)
    def _():
        slot = lax.rem(b, 2)
        other = 1 - slot
        prev = jnp.maximum(b - 1, 0)

        @pl.when(b == 0)
        def _():
            start_in(0, 0)

        @pl.when(b + 1 < n_used)
        def _():
            start_in(b + 1, other)

        @pl.when((b == 0) | (bexp_ref[prev] != bexp_ref[b]))
        def _():
            wg_bf[...] = wg_ref[...].astype(BF16)
            wu_bf[...] = wu_ref[...].astype(BF16)
            wd_bf[...] = wd_ref[...].astype(BF16)

        wait_in(slot)

        @pl.when(b >= 2)
        def _():
            wait_out(slot)

        xb = xbuf[slot].astype(BF16)
        a = _dot(xb, wg_bf[...])
        hid = (a * _sigmoid(a) * _dot(xb, wu_bf[...])).astype(BF16)
        ybuf[slot] = _dot(hid, wd_bf[...])
        start_out(b, slot)

        @pl.when(b == n_used - 1)
        def _():
            @pl.when(b >= 1)
            def _():
                wait_out(other)
            wait_out(slot)


def _experts(hs, w_gate, w_up, w_down, bexp, nused, src, dst, used_rows, n_out_rows):
    _, d = hs.shape
    _, _, de = w_gate.shape
    rb = EXPERT_TILE
    assert (n_out_rows - hs.shape[0]) % rb == 0
    wmap = lambda b, bexp, nused, src, dst, used: (bexp[b], 0, 0)
    grid_spec = pltpu.PrefetchScalarGridSpec(
        num_scalar_prefetch=5,
        grid=(bexp.shape[0],),
        in_specs=[pl.BlockSpec(memory_space=pl.ANY),
                  pl.BlockSpec((None, d, de), wmap),
                  pl.BlockSpec((None, d, de), wmap),
                  pl.BlockSpec((None, de, d), wmap)],
        out_specs=pl.BlockSpec(memory_space=pl.ANY),
        scratch_shapes=[pltpu.VMEM((2, rb, d), F32), pltpu.VMEM((2, rb, d), F32),
                        pltpu.VMEM((d, de), BF16), pltpu.VMEM((d, de), BF16),
                        pltpu.VMEM((de, d), BF16),
                        pltpu.SemaphoreType.DMA((2,)), pltpu.SemaphoreType.DMA((2,))],
    )
    return pl.pallas_call(
        _expert_kernel,
        grid_spec=grid_spec,
        out_shape=jax.ShapeDtypeStruct((n_out_rows, d), F32),
        compiler_params=pltpu.CompilerParams(dimension_semantics=("arbitrary",),
                                             vmem_limit_bytes=VMEM_LIMIT),
        name="experts",
    )(bexp, nused, src, dst, used_rows, hs, w_gate, w_up, w_down)


def _combine_kernel(x1_ref, p_ref, lrow_ref, gw_ref, g3_ref, wpg_ref, wpp_ref, gf_ref, ys_ref,
                    out_ref):
    ts = x1_ref.shape[0]
    ciota = lax.broadcasted_iota(jnp.int32, (ts, ys_ref.shape[0]), 1)
    pick = jnp.where(ciota == lrow_ref[:, 0:1], gw_ref[:, 0:1],
                     jnp.where(ciota == lrow_ref[:, 1:2], gw_ref[:, 1:2], 0.0))
    x2 = x1_ref[...] + _dot(pick.astype(BF16), ys_ref[...].astype(BF16))
    hp = _rms(x2, g3_ref[...]).astype(BF16)
    proj = _dot(p_ref[...].astype(BF16), wpp_ref[...])
    x3 = x2 + _sigmoid(_dot(hp, wpg_ref[...])) * proj
    out_ref[...] = _rms(x3, gf_ref[...])


def _combine(x1, p, lrow_t, gw_t, ys, g3, w_pg, w_pp, gf):
    t, d = x1.shape
    pd = p.shape[1]
    ts = SEQ_TILE
    tile = lambda i: (i, 0)
    const = lambda i: (0, 0)
    return pl.pallas_call(
        _combine_kernel,
        grid=(t // ts,),
        in_specs=[pl.BlockSpec((ts, d), tile),
                  pl.BlockSpec((ts, pd), tile),
                  pl.BlockSpec((ts, TOP_K), tile),
                  pl.BlockSpec((ts, TOP_K), tile),
                  pl.BlockSpec((1, d), const),
                  pl.BlockSpec((d, d), const),
                  pl.BlockSpec((pd, d), const),
                  pl.BlockSpec((1, d), const),
                  pl.BlockSpec((SORTED_ROWS, d), tile)],
        out_specs=pl.BlockSpec((ts, d), tile),
        out_shape=jax.ShapeDtypeStruct((t, d), F32),
        compiler_params=pltpu.CompilerParams(dimension_semantics=("arbitrary",),
                                             vmem_limit_bytes=VMEM_LIMIT),
        name="combine",
    )(x1, p, lrow_t, gw_t, g3, w_pg, w_pp, gf, ys)


def _piece_plan(n_pieces):
    n_tiles = n_pieces.shape[0]
    ppb = PIECES_PER_BLOCK
    tile_pieces = SORTED_ROWS // PIECE
    i32 = jnp.int32
    excl = lambda v, axis: jnp.cumsum(v, axis=axis, dtype=i32) - v
    base = (jnp.arange(n_tiles, dtype=i32) * tile_pieces)[:, None] + excl(n_pieces, 1)
    per_expert = n_pieces.T
    blocks = (jnp.sum(per_expert, axis=1, dtype=i32) + ppb - 1) // ppb
    first_block = excl(blocks, 0)
    n_used = jnp.sum(blocks, dtype=i32)

    max_blocks = (TOP_K * n_tiles * SEQ_TILE // PIECE + n_tiles * N_EXPERTS) // ppb + N_EXPERTS
    block_ids = jnp.arange(max_blocks, dtype=i32)
    blk = jnp.minimum(block_ids, n_used - 1)
    bexp = jnp.sum((first_block[None, :] <= blk[:, None]).astype(i32), axis=1) - 1
    strip_lo = jnp.take(excl(per_expert, 1), bexp, axis=0)[:, None, :]
    strip_len = jnp.take(per_expert, bexp, axis=0)[:, None, :]
    strip_base = jnp.take(base.T, bexp, axis=0)[:, None, :]
    local = ((block_ids - jnp.take(first_block, bexp)) * ppb)[:, None] + jnp.arange(ppb, dtype=i32)
    local = local[:, :, None]
    inside = (local >= strip_lo) & (local < strip_lo + strip_len)
    src = jnp.sum(jnp.where(inside, strip_base + local - strip_lo, 0), axis=2, dtype=i32).reshape(-1)
    real = jnp.any(inside, axis=2).reshape(-1)
    n_dump = N_EXPERTS * ppb
    pad_rank = jnp.minimum(jnp.cumsum(jnp.logical_not(real), dtype=i32) - 1, n_dump - 1)
    dst = jnp.where(real, src, n_tiles * tile_pieces + pad_rank)
    n_out_rows = (n_tiles * tile_pieces + n_dump) * PIECE
    return bexp, n_used.reshape(1), src, dst, n_out_rows


def _layer(x, p, mix_norm_g, w_in, conv_a_w, w_out_a, conv_b_w, conv_b_b, ln_b_g, ln_b_b, w_out_b,
           b_gate, w_o, ffn_norm_g, w_rg, b_rg, w_re, b_re, w_eg, w_eu, w_ed, ple_norm_g, w_pg, w_pp,
           out_norm_g):
    bsz, seq, d = x.shape
    t = bsz * seq
    row = lambda v: v.reshape(1, -1).astype(F32)
    w_r = jnp.zeros((ROUTER_ROWS, d), F32).at[0:N_GROUPS].set(w_rg.T).at[8:].set(w_re.T)
    b_r = jnp.zeros((ROUTER_ROWS,), F32).at[0:N_GROUPS].set(b_rg).at[8:].set(b_re)
    b_r = jnp.broadcast_to(b_r[:, None], (ROUTER_ROWS, LANES))

    x1, lrow, gw, n_pieces, hs = _mixer(
        x, row(mix_norm_g), w_in.astype(BF16), conv_a_w, w_out_a.astype(BF16), conv_b_w,
        row(conv_b_b), row(ln_b_g), row(ln_b_b), w_out_b.astype(BF16), row(b_gate),
        w_o.astype(BF16), row(ffn_norm_g), w_r, b_r)
    x1 = x1.reshape(t, d)

    n_pieces = n_pieces[:, :, 0]
    bexp, n_used, src, dst, n_out_rows = _piece_plan(n_pieces)
    used_rows = jnp.sum(n_pieces, axis=1, dtype=jnp.int32) * PIECE
    ys = _experts(hs, w_eg, w_eu, w_ed, bexp, n_used, src, dst, used_rows, n_out_rows)
    out = _combine(x1, p.reshape(t, -1), lrow.T, gw.T, ys, row(ple_norm_g),
                   w_pg.astype(BF16), w_pp.astype(BF16), row(out_norm_g))
    return out.reshape(bsz, seq, d)


def kernel(x, p, mix_norm_g, w_in, conv_a_w, w_out_a, conv_b_w, conv_b_b, ln_b_g, ln_b_b, w_out_b, b_gate, w_o, ffn_norm_g, w_router_group, b_router_group, w_router_expert, b_router_expert, w_exp_gate, w_exp_up, w_exp_down, ple_norm_g, w_ple_gate, w_ple_proj, final_norm_g):
    assert w_in.shape[0] == 1, "the combine kernel fuses the final RMSNorm: single layer only"
    return _layer(x, p[0], mix_norm_g[0], w_in[0], conv_a_w[0], w_out_a[0], conv_b_w[0],
                  conv_b_b[0], ln_b_g[0], ln_b_b[0], w_out_b[0], b_gate[0], w_o[0], ffn_norm_g[0],
                  w_router_group[0], b_router_group[0], w_router_expert[0], b_router_expert[0],
                  w_exp_gate[0], w_exp_up[0], w_exp_down[0], ple_norm_g[0], w_ple_gate[0],
                  w_ple_proj[0], final_norm_g)
```

```python
import functools

import jax
import jax.numpy as jnp
from jax import lax
from jax.experimental import pallas as pl
from jax.experimental.pallas import tpu as pltpu

F32 = jnp.float32
BF16 = jnp.bfloat16
EPS = 1e-6

LANES = 128
SUBLANES = 8
N_GROUPS = 4
EXPERTS_PER_GROUP = 8
N_EXPERTS = N_GROUPS * EXPERTS_PER_GROUP
TOP_K = 2
ROUTER_ROWS = 8 + N_EXPERTS

SEQ_TILE = 512
CONV_ROWS = 128
MXU_COLS = 256
HALO_A = 8
HALO_B = 32
TAIL_ROWS = 128
PIECE = SUBLANES
EXPERT_TILE = 512
PIECES_PER_BLOCK = EXPERT_TILE // PIECE
SORTED_ROWS = TOP_K * SEQ_TILE + N_EXPERTS * PIECE
VMEM_LIMIT = 56 * 1024 * 1024
MIXER_VMEM_LIMIT = 60 * 1024 * 1024


def _sigmoid(v):
    return 1.0 / (1.0 + jnp.exp(-v))


def _rms(v, g):
    return v * lax.rsqrt(jnp.mean(v * v, axis=-1, keepdims=True) + EPS) * g


def _dot(a, b):
    return jnp.dot(a, b, preferred_element_type=F32)


def _conv_steps(buf, w_ref, width, halo, rows, emit):
    steps = []
    for j in range(buf.shape[0]):
        for base in range(0, rows, CONV_ROWS):
            def step(j=j, base=base):
                lanes = slice(j * LANES, (j + 1) * LANES)
                acc = None
                for k in range(width):
                    start = base + halo - (width - 1) + k
                    term = w_ref[k:k + 1, lanes] * buf[j, start:start + CONV_ROWS, :]
                    acc = term if acc is None else acc + term
                emit(slice(base, base + CONV_ROWS), lanes, acc)
            steps.append((width * CONV_ROWS // 8, step))
    return steps


def _packed_conv_steps(packed, w_ref, width, halo, rows, emit):
    group = 2 * SUBLANES
    steps = []
    for j in range(packed.shape[1]):
        for base in range(0, rows, CONV_ROWS):
            def step(j=j, base=base):
                accs = [None] * (CONV_ROWS // group)
                for k in range(width):
                    shift = k % 2
                    tap = w_ref[k, j]
                    for g in range(len(accs)):
                        word = (base + g * group + halo - (width - 1) + k - shift) // 2
                        vals = pltpu.bitcast(packed[shift, j, word:word + SUBLANES, :], BF16)
                        accs[g] = vals * tap if accs[g] is None else accs[g] + vals * tap
                for g, acc in enumerate(accs):
                    lo = base + g * group
                    emit(slice(lo, lo + group), slice(j * LANES, (j + 1) * LANES), acc.astype(F32))
            steps.append((width * CONV_ROWS // 16, step))
    return steps


def _interleave(mxu_steps, vpu_steps):
    total_m = sum(c for c, _ in mxu_steps)
    total_v = sum(c for c, _ in vpu_steps)
    done_m = 0
    done_v = 0
    pending = list(vpu_steps)
    for cost, step in mxu_steps:
        step()
        done_m += cost
        while pending and done_v * total_m < done_m * total_v:
            c, vstep = pending.pop(0)
            vstep()
            done_v += c
    for _, vstep in pending:
        vstep()


def _mixer_kernel(x_ref, g1_ref, win_ref, caw_ref, woa_ref, cbw_ref, cbb_ref, lng_ref, lnb_ref,
                  wob_ref, bg_ref, wo_ref, g2_ref, wr_ref, br_ref,
                  x1_ref, lrow_ref, gw_ref, np_ref, hs_ref,
                  abuf, bbuf, bpk, cbuf, pbuf, zbuf, swbuf, *, width_a, width_b, d_model, wa, wb):
    ts = x_ref.shape[0]

    @pl.when(pl.program_id(1) == 0)
    def _():
        abuf[:, 0:HALO_A, :] = jnp.zeros((abuf.shape[0], HALO_A, LANES), F32)
        bbuf[:, 0:HALO_B, :] = jnp.zeros((bbuf.shape[0], HALO_B, LANES), F32)
        bbuf[:, HALO_B + ts:, :] = jnp.zeros((bbuf.shape[0], SUBLANES, LANES), F32)

    h = _rms(x_ref[...], g1_ref[...]).astype(BF16)
    col_b, col_c, col_x = 0, wa, 2 * wa
    col_v, col_g, col_gate = 3 * wa, 3 * wa + wb, 3 * wa + 2 * wb

    def proj(col, q):
        return _dot(h, win_ref[:, col + q * MXU_COLS:col + (q + 1) * MXU_COLS])

    raw_cols = ((col_c, wa), (col_x, wa), (col_b, wa), (col_gate, 2 * d_model))
    pb_c, pb_x, pb_b, pb_g = 0, wa, 2 * wa, 3 * wa
    raw_steps = []
    off = 0
    for col, width in raw_cols:
        for q in range(width // MXU_COLS):
            def raw_step(col=col, q=q, off=off):
                lo = off + q * MXU_COLS
                pbuf[:, lo:lo + MXU_COLS] = proj(col, q)
            raw_steps.append((1, raw_step))
        off += width

    def emit_b(rows, lanes, acc):
        cbuf[rows, lanes] = acc

    conv_steps = _packed_conv_steps(bpk, cbw_ref, width_b, HALO_B, ts, emit_b)

    n_glu = wb // MXU_COLS
    lanes_per_glu = MXU_COLS // LANES
    for q in range(n_glu):
        u = proj(col_v, q) * _sigmoid(proj(col_g, q))
        for jj in range(lanes_per_glu):
            j = q * lanes_per_glu + jj
            bbuf[j, HALO_B:HALO_B + ts, :] = u[:, jj * LANES:(jj + 1) * LANES]
            for shift in range(2):
                rows_f32 = bbuf[j, shift:shift + HALO_B + ts, :]
                bpk[shift, j] = pltpu.bitcast(rows_f32.astype(BF16), jnp.uint32)
            bbuf[j, 0:HALO_B, :] = bbuf[j, ts:ts + HALO_B, :]
        _interleave(raw_steps[q * len(raw_steps) // n_glu:(q + 1) * len(raw_steps) // n_glu],
                    conv_steps[q * len(conv_steps) // n_glu:(q + 1) * len(conv_steps) // n_glu])

    for j in range(wa // LANES):
        abuf[j, HALO_A:HALO_A + ts, :] = (pbuf[:, pb_c + j * LANES:pb_c + (j + 1) * LANES]
                                           * pbuf[:, pb_x + j * LANES:pb_x + (j + 1) * LANES])

    def emit_a(rows, lanes, acc):
        zbuf[rows, lanes] = (pbuf[rows, pb_b + lanes.start:pb_b + lanes.stop] * acc).astype(BF16)

    for _, step in _conv_steps(abuf, caw_ref, width_a, HALO_A, ts, emit_a):
        step()
    abuf[:, 0:HALO_A, :] = abuf[:, ts:ts + HALO_A, :]

    for r in range(0, ts, TAIL_ROWS):
        rows = slice(r, r + TAIL_ROWS)
        c = cbuf[rows, :] + cbb_ref[...]
        mu = jnp.mean(c, axis=-1, keepdims=True)
        cc = c - mu
        var = jnp.mean(cc * cc, axis=-1, keepdims=True)
        ln = cc * lax.rsqrt(var + EPS) * lng_ref[...] + lnb_ref[...]
        swbuf[rows, :] = (ln * _sigmoid(ln)).astype(BF16)
        gcols = slice(pb_g, pb_g + 2 * d_model)
        pbuf[rows, gcols] = _sigmoid(pbuf[rows, gcols] + bg_ref[...])
    mix = (pbuf[:, pb_g:pb_g + d_model] * _dot(zbuf[...], woa_ref[...])
           + pbuf[:, pb_g + d_model:pb_g + 2 * d_model] * _dot(swbuf[...], wob_ref[...]))

    x1 = x_ref[...] + _dot(mix.astype(BF16), wo_ref[...])
    x1_ref[...] = x1

    h2 = _rms(x1, g2_ref[...])
    logits = lax.dot_general(wr_ref[...], h2, (((1,), (1,)), ((), ())),
                             preferred_element_type=F32) + br_ref[:, 0:1]
    gl = logits[0:N_GROUPS, :]
    gmax = jnp.max(gl, axis=0, keepdims=True)
    giota = lax.broadcasted_iota(jnp.int32, gl.shape, 0)
    grp = jnp.min(jnp.where(gl == gmax, giota, N_GROUPS), axis=0, keepdims=True)
    p_grp = 1.0 / jnp.sum(jnp.exp(gl - gmax), axis=0, keepdims=True)

    el = logits[8:8 + N_EXPERTS, :]
    eiota = lax.broadcasted_iota(jnp.int32, el.shape, 0)
    neg = jnp.float32(-jnp.inf)
    m0 = jnp.where(lax.div(eiota, EXPERTS_PER_GROUP) == grp, el, neg)
    v1 = jnp.max(m0, axis=0, keepdims=True)
    i1 = jnp.min(jnp.where(m0 == v1, eiota, N_EXPERTS), axis=0, keepdims=True)
    m1 = jnp.where(eiota == i1, neg, m0)
    v2 = jnp.max(m1, axis=0, keepdims=True)
    i2 = jnp.min(jnp.where(m1 == v2, eiota, N_EXPERTS), axis=0, keepdims=True)
    e2 = jnp.exp(v2 - v1)
    w1 = p_grp / (1.0 + e2)
    w2 = p_grp * e2 / (1.0 + e2)

    oh1 = eiota == i1
    oh2 = eiota == i2
    used = jnp.where(oh1, 1.0, jnp.where(oh2, 1.0, 0.0))
    rio = lax.broadcasted_iota(jnp.int32, (ts, ts), 0)
    cio = lax.broadcasted_iota(jnp.int32, (ts, ts), 1)
    before = jnp.where(rio < cio, 1.0, 0.0).astype(BF16)
    in_strip = _dot(used.astype(BF16), before)
    pieces = jnp.floor((jnp.sum(used, axis=1, keepdims=True) + (PIECE - 1.0)) * (1.0 / PIECE))
    er = lax.broadcasted_iota(jnp.int32, (N_EXPERTS, N_EXPERTS), 0)
    ec = lax.broadcasted_iota(jnp.int32, (N_EXPERTS, N_EXPERTS), 1)
    earlier = jnp.where(ec < er, 1.0, 0.0)
    strip_start = _dot(earlier, jnp.broadcast_to(pieces * PIECE, (N_EXPERTS, LANES)))[:, 0:1]
    row = strip_start + in_strip
    r1 = jnp.sum(jnp.where(oh1, row, 0.0), axis=0, keepdims=True)
    r2 = jnp.sum(jnp.where(oh2, row, 0.0), axis=0, keepdims=True)

    lrow = jnp.concatenate([r1, r2], axis=0).astype(jnp.int32)
    lrow_ref[...] = lrow
    gw_ref[...] = jnp.concatenate([w1, w2], axis=0)
    np_ref[...] = jnp.broadcast_to(pieces, np_ref.shape).astype(jnp.int32)

    riota = lax.broadcasted_iota(jnp.int32, (hs_ref.shape[0], ts), 0)
    pick = jnp.where(riota == lrow[0:1, :], 1.0, jnp.where(riota == lrow[1:2, :], 1.0, 0.0))
    hs_ref[...] = _dot(pick.astype(BF16), h2.astype(BF16))


def _const_spec(shape):
    return pl.BlockSpec(shape, lambda *_: (0,) * len(shape), pipeline_mode=pl.Buffered(1))


def _mixer(x, g1, w_in, conv_a_w, w_out_a, conv_b_w, conv_b_b, ln_g, ln_b, w_out_b, b_gate, w_o,
           g2, w_r, b_r):
    bsz, seq, d = x.shape
    wa = w_out_a.shape[0]
    wb = w_out_b.shape[0]
    ts = SEQ_TILE
    n_s = seq // ts
    t = bsz * seq
    taps_b = conv_b_w.astype(BF16).reshape(conv_b_w.shape[0], wb // LANES, 1, LANES)
    taps_b = jnp.broadcast_to(taps_b, taps_b.shape[:2] + (2 * SUBLANES, LANES))
    consts = (g1, w_in, conv_a_w, w_out_a, taps_b, conv_b_b, ln_g, ln_b, w_out_b, b_gate, w_o,
              g2, w_r, b_r)
    tok_spec = pl.BlockSpec((TOP_K, ts), lambda b, s: (0, b * n_s + s))
    return pl.pallas_call(
        functools.partial(_mixer_kernel, width_a=conv_a_w.shape[0], width_b=conv_b_w.shape[0],
                          d_model=d, wa=wa, wb=wb),
        grid=(bsz, n_s),
        in_specs=[pl.BlockSpec((None, ts, d), lambda b, s: (b, s, 0))]
                 + [_const_spec(c.shape) for c in consts],
        out_specs=[pl.BlockSpec((None, ts, d), lambda b, s: (b, s, 0)),
                   tok_spec, tok_spec,
                   pl.BlockSpec((None, N_EXPERTS, LANES), lambda b, s: (b * n_s + s, 0, 0)),
                   pl.BlockSpec((SORTED_ROWS, d), lambda b, s: (b * n_s + s, 0))],
        out_shape=[jax.ShapeDtypeStruct((bsz, seq, d), F32),
                   jax.ShapeDtypeStruct((TOP_K, t), jnp.int32),
                   jax.ShapeDtypeStruct((TOP_K, t), F32),
                   jax.ShapeDtypeStruct((t // ts, N_EXPERTS, LANES), jnp.int32),
                   jax.ShapeDtypeStruct((t // ts * SORTED_ROWS, d), F32)],
        scratch_shapes=[pltpu.VMEM((wa // LANES, HALO_A + ts, LANES), F32),
                        pltpu.VMEM((wb // LANES, HALO_B + ts + SUBLANES, LANES), F32),
                        pltpu.VMEM((2, wb // LANES, (HALO_B + ts) // 2, LANES), jnp.uint32),
                        pltpu.VMEM((ts, wb), F32),
                        pltpu.VMEM((ts, 3 * wa + 2 * d), F32),
                        pltpu.VMEM((ts, wa), BF16),
                        pltpu.VMEM((ts, wb), BF16)],
        compiler_params=pltpu.CompilerParams(
            dimension_semantics=("arbitrary", "arbitrary"), vmem_limit_bytes=MIXER_VMEM_LIMIT),
        name="mixer",
    )(x, *consts)


def _zero_unwritten(used_ref, ys_ref, zero, sem):
    n_tiles = used_ref.shape[0]
    rb = zero.shape[0]

    def fill(row, n_rows):
        return pltpu.make_async_copy(zero.at[pl.ds(0, n_rows), :],
                                     ys_ref.at[pl.ds(row, n_rows), :], sem)

    dump = [fill(r, rb) for r in range(n_tiles * SORTED_ROWS, ys_ref.shape[0], rb)]

    def tails(wait):
        def body(i, carry):
            used = used_ref[i]
            tail = SORTED_ROWS - used
            row = i * SORTED_ROWS + used
            size = SORTED_ROWS - TOP_K * SEQ_TILE
            while size >= PIECE:
                @pl.when((tail & size) != 0)
                def _(row=row, size=size):
                    c = fill(pl.multiple_of(row, PIECE), size)
                    c.wait() if wait else c.start()
                row = row + (tail & size)
                size //= 2
            return carry
        lax.fori_loop(0, n_tiles, body, 0)

    zero[...] = jnp.zeros(zero.shape, F32)
    for c in dump:
        c.start()
    tails(False)
    for c in dump:
        c.wait()
    tails(True)


def _expert_kernel(bexp_ref, nused_ref, src_ref, dst_ref, used_ref, hs_ref, wg_ref, wu_ref, wd_ref,
                   ys_ref, xbuf, ybuf, wg_bf, wu_bf, wd_bf, sem_in, sem_out):
    b = pl.program_id(0)
    n_used = nused_ref[0]
    rb = xbuf.shape[1]

    @pl.when(b == 0)
    def _():
        _zero_unwritten(used_ref, ys_ref, ybuf.at[1], sem_out.at[1])

    def piece_copies(table_ref, blk, hbm_ref, buf, slot, sem, inbound):
        copies = []
        for j in range(PIECES_PER_BLOCK):
            row = pl.multiple_of(table_ref[blk * PIECES_PER_BLOCK + j] * PIECE, PIECE)
            hbm = hbm_ref.at[pl.ds(row, PIECE), :]
            vmem = buf.at[slot, pl.ds(j * PIECE, PIECE), :]
            copies.append(pltpu.make_async_copy(hbm, vmem, sem.at[slot]) if inbound
                          else pltpu.make_async_copy(vmem, hbm, sem.at[slot]))
        return copies

    def start_in(blk, slot):
        for c in piece_copies(src_ref, blk, hs_ref, xbuf, slot, sem_in, True):
            c.start()

    def start_out(blk, slot):
        for c in piece_copies(dst_ref, blk, ys_ref, ybuf, slot, sem_out, False):
            c.start()

    def wait_in(slot):
        pltpu.make_async_copy(hs_ref.at[pl.ds(0, rb), :], xbuf.at[slot], sem_in.at[slot]).wait()

    def wait_out(slot):
        pltpu.make_async_copy(ybuf.at[slot], ys_ref.at[pl.ds(0, rb), :], sem_out.at[slot]).wait()

    @pl.when(b < n_used)
    def _():
        slot = lax.rem(b, 2)
        other = 1 - slot
        prev = jnp.maximum(b - 1, 0)

        @pl.when(b == 0)
        def _():
            start_in(0, 0)

        @pl.when(b + 1 < n_used)
        def _():
            start_in(b + 1, other)

        @pl.when((b == 0) | (bexp_ref[prev] != bexp_ref[b]))
        def _():
            wg_bf[...] = wg_ref[...].astype(BF16)
            wu_bf[...] = wu_ref[...].astype(BF16)
            wd_bf[...] = wd_ref[...].astype(BF16)

        wait_in(slot)

        @pl.when(b >= 2)
        def _():
            wait_out(slot)

        xb = xbuf[slot].astype(BF16)
        a = _dot(xb, wg_bf[...])
        hid = (a * _sigmoid(a) * _dot(xb, wu_bf[...])).astype(BF16)
        ybuf[slot] = _dot(hid, wd_bf[...])
        start_out(b, slot)

        @pl.when(b == n_used - 1)
        def _():
            @pl.when(b >= 1)
            def _():
                wait_out(other)
            wait_out(slot)


def _experts(hs, w_gate, w_up, w_down, bexp, nused, src, dst, used_rows, n_out_rows):
    _, d = hs.shape
    _, _, de = w_gate.shape
    rb = EXPERT_TILE
    assert (n_out_rows - hs.shape[0]) % rb == 0
    wmap = lambda b, bexp, nused, src, dst, used: (bexp[b], 0, 0)
    grid_spec = pltpu.PrefetchScalarGridSpec(
        num_scalar_prefetch=5,
        grid=(bexp.shape[0],),
        in_specs=[pl.BlockSpec(memory_space=pl.ANY),
                  pl.BlockSpec((None, d, de), wmap),
                  pl.BlockSpec((None, d, de), wmap),
                  pl.BlockSpec((None, de, d), wmap)],
        out_specs=pl.BlockSpec(memory_space=pl.ANY),
        scratch_shapes=[pltpu.VMEM((2, rb, d), F32), pltpu.VMEM((2, rb, d), F32),
                        pltpu.VMEM((d, de), BF16), pltpu.VMEM((d, de), BF16),
                        pltpu.VMEM((de, d), BF16),
                        pltpu.SemaphoreType.DMA((2,)), pltpu.SemaphoreType.DMA((2,))],
    )
    return pl.pallas_call(
        _expert_kernel,
        grid_spec=grid_spec,
        out_shape=jax.ShapeDtypeStruct((n_out_rows, d), F32),
        compiler_params=pltpu.CompilerParams(dimension_semantics=("arbitrary",),
                                             vmem_limit_bytes=VMEM_LIMIT),
        name="experts",
    )(bexp, nused, src, dst, used_rows, hs, w_gate, w_up, w_down)


def _combine_kernel(x1_ref, p_ref, lrow_ref, gw_ref, g3_ref, wpg_ref, wpp_ref, gf_ref, ys_ref,
                    out_ref):
    ts = x1_ref.shape[0]
    riota = lax.broadcasted_iota(jnp.int32, (ys_ref.shape[0], ts), 0)
    pick = jnp.where(riota == lrow_ref[0:1, :], gw_ref[0:1, :],
                     jnp.where(riota == lrow_ref[1:2, :], gw_ref[1:2, :], 0.0))
    moe = lax.dot_general(pick.astype(BF16), ys_ref[...].astype(BF16), (((0,), (0,)), ((), ())),
                          preferred_element_type=F32)
    x2 = x1_ref[...] + moe
    hp = _rms(x2, g3_ref[...]).astype(BF16)
    proj = _dot(p_ref[...].astype(BF16), wpp_ref[...])
    x3 = x2 + _sigmoid(_dot(hp, wpg_ref[...])) * proj
    out_ref[...] = _rms(x3, gf_ref[...])


def _combine(x1, p, lrow, gw, ys, g3, w_pg, w_pp, gf):
    t, d = x1.shape
    pd = p.shape[1]
    ts = SEQ_TILE
    tile = lambda i: (i, 0)
    const = lambda i: (0, 0)
    return pl.pallas_call(
        _combine_kernel,
        grid=(t // ts,),
        in_specs=[pl.BlockSpec((ts, d), tile),
                  pl.BlockSpec((ts, pd), tile),
                  pl.BlockSpec((TOP_K, ts), lambda i: (0, i)),
                  pl.BlockSpec((TOP_K, ts), lambda i: (0, i)),
                  pl.BlockSpec((1, d), const),
                  pl.BlockSpec((d, d), const),
                  pl.BlockSpec((pd, d), const),
                  pl.BlockSpec((1, d), const),
                  pl.BlockSpec((SORTED_ROWS, d), tile)],
        out_specs=pl.BlockSpec((ts, d), tile),
        out_shape=jax.ShapeDtypeStruct((t, d), F32),
        compiler_params=pltpu.CompilerParams(dimension_semantics=("arbitrary",),
                                             vmem_limit_bytes=VMEM_LIMIT),
        name="combine",
    )(x1, p, lrow, gw, g3, w_pg, w_pp, gf, ys)


def _piece_plan(n_pieces):
    n_tiles = n_pieces.shape[0]
    ppb = PIECES_PER_BLOCK
    tile_pieces = SORTED_ROWS // PIECE
    i32 = jnp.int32
    excl = lambda v, axis: jnp.cumsum(v, axis=axis, dtype=i32) - v
    base = (jnp.arange(n_tiles, dtype=i32) * tile_pieces)[:, None] + excl(n_pieces, 1)
    per_expert = n_pieces.T
    blocks = (jnp.sum(per_expert, axis=1, dtype=i32) + ppb - 1) // ppb
    first_block = excl(blocks, 0)
    n_used = jnp.sum(blocks, dtype=i32)

    max_blocks = (TOP_K * n_tiles * SEQ_TILE // PIECE + n_tiles * N_EXPERTS) // ppb + N_EXPERTS
    block_ids = jnp.arange(max_blocks, dtype=i32)
    blk = jnp.minimum(block_ids, n_used - 1)
    bexp = jnp.sum((first_block[None, :] <= blk[:, None]).astype(i32), axis=1) - 1
    strip_lo = jnp.take(excl(per_expert, 1), bexp, axis=0)[:, None, :]
    strip_len = jnp.take(per_expert, bexp, axis=0)[:, None, :]
    strip_base = jnp.take(base.T, bexp, axis=0)[:, None, :]
    local = ((block_ids - jnp.take(first_block, bexp)) * ppb)[:, None] + jnp.arange(ppb, dtype=i32)
    local = local[:, :, None]
    inside = (local >= strip_lo) & (local < strip_lo + strip_len)
    src = jnp.sum(jnp.where(inside, strip_base + local - strip_lo, 0), axis=2, dtype=i32).reshape(-1)
    real = jnp.any(inside, axis=2).reshape(-1)
    n_dump = N_EXPERTS * ppb
    pad_rank = jnp.minimum(jnp.cumsum(jnp.logical_not(real), dtype=i32) - 1, n_dump - 1)
    dst = jnp.where(real, src, n_tiles * tile_pieces + pad_rank)
    n_out_rows = (n_tiles * tile_pieces + n_dump) * PIECE
    return bexp, n_used.reshape(1), src, dst, n_out_rows


def _layer(x, p, mix_norm_g, w_in, conv_a_w, w_out_a, conv_b_w, conv_b_b, ln_b_g, ln_b_b, w_out_b,
           b_gate, w_o, ffn_norm_g, w_rg, b_rg, w_re, b_re, w_eg, w_eu, w_ed, ple_norm_g, w_pg, w_pp,
           out_norm_g):
    bsz, seq, d = x.shape
    t = bsz * seq
    row = lambda v: v.reshape(1, -1).astype(F32)
    w_r = jnp.zeros((ROUTER_ROWS, d), F32).at[0:N_GROUPS].set(w_rg.T).at[8:].set(w_re.T)
    b_r = jnp.zeros((ROUTER_ROWS,), F32).at[0:N_GROUPS].set(b_rg).at[8:].set(b_re)
    b_r = jnp.broadcast_to(b_r[:, None], (ROUTER_ROWS, LANES))

    x1, lrow, gw, n_pieces, hs = _mixer(
        x, row(mix_norm_g), w_in.astype(BF16), conv_a_w, w_out_a.astype(BF16), conv_b_w,
        row(conv_b_b), row(ln_b_g), row(ln_b_b), w_out_b.astype(BF16), row(b_gate),
        w_o.astype(BF16), row(ffn_norm_g), w_r, b_r)
    x1 = x1.reshape(t, d)

    n_pieces = n_pieces[:, :, 0]
    bexp, n_used, src, dst, n_out_rows = _piece_plan(n_pieces)
    used_rows = jnp.sum(n_pieces, axis=1, dtype=jnp.int32) * PIECE
    ys = _experts(hs, w_eg, w_eu, w_ed, bexp, n_used, src, dst, used_rows, n_out_rows)
    out = _combine(x1, p.reshape(t, -1), lrow, gw, ys, row(ple_norm_g),
                   w_pg.astype(BF16), w_pp.astype(BF16), row(out_norm_g))
    return out.reshape(bsz, seq, d)


def kernel(x, p, mix_norm_g, w_in, conv_a_w, w_out_a, conv_b_w, conv_b_b, ln_b_g, ln_b_b, w_out_b, b_gate, w_o, ffn_norm_g, w_router_group, b_router_group, w_router_expert, b_router_expert, w_exp_gate, w_exp_up, w_exp_down, ple_norm_g, w_ple_gate, w_ple_proj, final_norm_g):
    assert w_in.shape[0] == 1, "the combine kernel fuses the final RMSNorm: single layer only"
    return _layer(x, p[0], mix_norm_g[0], w_in[0], conv_a_w[0], w_out_a[0], conv_b_w[0],
                  conv_b_b[0], ln_b_g[0], ln_b_b[0], w_out_b[0], b_gate[0], w_o[0], ffn_norm_g[0],
                  w_router_group[0], b_router_group[0], w_router_expert[0], b_router_expert[0],
                  w_exp_gate[0], w_exp_up[0], w_exp_down[0], ple_norm_g[0], w_ple_gate[0],
                  w_ple_proj[0], final_norm_g)
```

```python
import functools

import jax
import jax.numpy as jnp
from jax import lax
from jax.experimental import pallas as pl
from jax.experimental.pallas import tpu as pltpu

F32 = jnp.float32
BF16 = jnp.bfloat16
EPS = 1e-6

LANES = 128
SUBLANES = 8
N_GROUPS = 4
EXPERTS_PER_GROUP = 8
N_EXPERTS = N_GROUPS * EXPERTS_PER_GROUP
TOP_K = 2
ROUTER_ROWS = 8 + N_EXPERTS

SEQ_TILE = 512
CONV_ROWS = 128
MXU_COLS = 256
HALO_A = 8
HALO_B = 32
TAIL_ROWS = 128
PIECE = SUBLANES
EXPERT_TILE = 512
PIECES_PER_BLOCK = EXPERT_TILE // PIECE
SORTED_ROWS = TOP_K * SEQ_TILE + N_EXPERTS * PIECE
VMEM_LIMIT = 56 * 1024 * 1024
MIXER_VMEM_LIMIT = 60 * 1024 * 1024


def _sigmoid(v):
    return 1.0 / (1.0 + jnp.exp(-v))


def _rms(v, g):
    return v * lax.rsqrt(jnp.mean(v * v, axis=-1, keepdims=True) + EPS) * g


def _dot(a, b):
    return jnp.dot(a, b, preferred_element_type=F32)


def _conv_steps(buf, w_ref, width, halo, rows, emit):
    steps = []
    for j in range(buf.shape[0]):
        for base in range(0, rows, CONV_ROWS):
            def step(j=j, base=base):
                lanes = slice(j * LANES, (j + 1) * LANES)
                acc = None
                for k in range(width):
                    start = base + halo - (width - 1) + k
                    term = w_ref[k:k + 1, lanes] * buf[j, start:start + CONV_ROWS, :]
                    acc = term if acc is None else acc + term
                emit(slice(base, base + CONV_ROWS), lanes, acc)
            steps.append((width * CONV_ROWS // 8, step))
    return steps


def _packed_conv_steps(packed, w_ref, width, halo, rows, emit):
    group = 2 * SUBLANES
    steps = []
    for j in range(packed.shape[1]):
        for base in range(0, rows, CONV_ROWS):
            def step(j=j, base=base):
                accs = [None] * (CONV_ROWS // group)
                for k in range(width):
                    shift = k % 2
                    tap = w_ref[k, j]
                    for g in range(len(accs)):
                        word = (base + g * group + halo - (width - 1) + k - shift) // 2
                        vals = pltpu.bitcast(packed[shift, j, word:word + SUBLANES, :], BF16)
                        accs[g] = vals * tap if accs[g] is None else accs[g] + vals * tap
                for g, acc in enumerate(accs):
                    lo = base + g * group
                    emit(slice(lo, lo + group), slice(j * LANES, (j + 1) * LANES), acc.astype(F32))
            steps.append((width * CONV_ROWS // 16, step))
    return steps


def _interleave(mxu_steps, vpu_steps):
    total_m = sum(c for c, _ in mxu_steps)
    total_v = sum(c for c, _ in vpu_steps)
    done_m = 0
    done_v = 0
    pending = list(vpu_steps)
    for cost, step in mxu_steps:
        step()
        done_m += cost
        while pending and done_v * total_m < done_m * total_v:
            c, vstep = pending.pop(0)
            vstep()
            done_v += c
    for _, vstep in pending:
        vstep()


def _mixer_kernel(x_ref, g1_ref, win_ref, caw_ref, woa_ref, cbw_ref, cbb_ref, lng_ref, lnb_ref,
                  wob_ref, bg_ref, wo_ref, g2_ref, wr_ref, br_ref, before_ref,
                  x1_ref, lrow_ref, gw_ref, np_ref, hs_ref,
                  abuf, bbuf, bpk, cbuf, pbuf, zbuf, swbuf, *, width_a, width_b, d_model, wa, wb):
    ts = x_ref.shape[0]

    @pl.when(pl.program_id(1) == 0)
    def _():
        abuf[:, 0:HALO_A, :] = jnp.zeros((abuf.shape[0], HALO_A, LANES), F32)
        bbuf[:, 0:HALO_B, :] = jnp.zeros((bbuf.shape[0], HALO_B, LANES), F32)
        bbuf[:, HALO_B + ts:, :] = jnp.zeros((bbuf.shape[0], SUBLANES, LANES), F32)

    h = _rms(x_ref[...], g1_ref[...]).astype(BF16)
    col_b, col_c, col_x = 0, wa, 2 * wa
    col_v, col_g, col_gate = 3 * wa, 3 * wa + wb, 3 * wa + 2 * wb

    def proj(col, q):
        return _dot(h, win_ref[:, col + q * MXU_COLS:col + (q + 1) * MXU_COLS])

    raw_cols = ((col_c, wa), (col_x, wa), (col_b, wa), (col_gate, 2 * d_model))
    pb_c, pb_x, pb_b, pb_g = 0, wa, 2 * wa, 3 * wa
    raw_steps = []
    off = 0
    for col, width in raw_cols:
        for q in range(width // MXU_COLS):
            def raw_step(col=col, q=q, off=off):
                lo = off + q * MXU_COLS
                pbuf[:, lo:lo + MXU_COLS] = proj(col, q)
            raw_steps.append((1, raw_step))
        off += width

    def emit_b(rows, lanes, acc):
        cbuf[rows, lanes] = acc

    conv_steps = _packed_conv_steps(bpk, cbw_ref, width_b, HALO_B, ts, emit_b)

    n_glu = wb // MXU_COLS
    lanes_per_glu = MXU_COLS // LANES
    for q in range(n_glu):
        u = proj(col_v, q) * _sigmoid(proj(col_g, q))
        for jj in range(lanes_per_glu):
            j = q * lanes_per_glu + jj
            bbuf[j, HALO_B:HALO_B + ts, :] = u[:, jj * LANES:(jj + 1) * LANES]
            for shift in range(2):
                rows_f32 = bbuf[j, shift:shift + HALO_B + ts, :]
                bpk[shift, j] = pltpu.bitcast(rows_f32.astype(BF16), jnp.uint32)
            bbuf[j, 0:HALO_B, :] = bbuf[j, ts:ts + HALO_B, :]
        _interleave(raw_steps[q * len(raw_steps) // n_glu:(q + 1) * len(raw_steps) // n_glu],
                    conv_steps[q * len(conv_steps) // n_glu:(q + 1) * len(conv_steps) // n_glu])

    for j in range(wa // LANES):
        abuf[j, HALO_A:HALO_A + ts, :] = (pbuf[:, pb_c + j * LANES:pb_c + (j + 1) * LANES]
                                           * pbuf[:, pb_x + j * LANES:pb_x + (j + 1) * LANES])

    def emit_a(rows, lanes, acc):
        zbuf[rows, lanes] = (pbuf[rows, pb_b + lanes.start:pb_b + lanes.stop] * acc).astype(BF16)

    for _, step in _conv_steps(abuf, caw_ref, width_a, HALO_A, ts, emit_a):
        step()
    abuf[:, 0:HALO_A, :] = abuf[:, ts:ts + HALO_A, :]

    for r in range(0, ts, TAIL_ROWS):
        rows = slice(r, r + TAIL_ROWS)
        c = cbuf[rows, :] + cbb_ref[...]
        mu = jnp.mean(c, axis=-1, keepdims=True)
        cc = c - mu
        var = jnp.mean(cc * cc, axis=-1, keepdims=True)
        ln = cc * lax.rsqrt(var + EPS) * lng_ref[...] + lnb_ref[...]
        swbuf[rows, :] = (ln * _sigmoid(ln)).astype(BF16)
        gcols = slice(pb_g, pb_g + 2 * d_model)
        pbuf[rows, gcols] = _sigmoid(pbuf[rows, gcols] + bg_ref[...])
    mix = (pbuf[:, pb_g:pb_g + d_model] * _dot(zbuf[...], woa_ref[...])
           + pbuf[:, pb_g + d_model:pb_g + 2 * d_model] * _dot(swbuf[...], wob_ref[...]))

    x1 = x_ref[...] + _dot(mix.astype(BF16), wo_ref[...])
    x1_ref[...] = x1

    h2 = _rms(x1, g2_ref[...])
    logits = lax.dot_general(wr_ref[...], h2, (((1,), (1,)), ((), ())),
                             preferred_element_type=F32) + br_ref[:, 0:1]
    gl = logits[0:N_GROUPS, :]
    gmax = jnp.max(gl, axis=0, keepdims=True)
    giota = lax.broadcasted_iota(jnp.int32, gl.shape, 0)
    grp = jnp.min(jnp.where(gl == gmax, giota, N_GROUPS), axis=0, keepdims=True)
    p_grp = 1.0 / jnp.sum(jnp.exp(gl - gmax), axis=0, keepdims=True)

    el = logits[8:8 + N_EXPERTS, :]
    eiota = lax.broadcasted_iota(jnp.int32, el.shape, 0)
    neg = jnp.float32(-jnp.inf)
    m0 = jnp.where(lax.div(eiota, EXPERTS_PER_GROUP) == grp, el, neg)
    v1 = jnp.max(m0, axis=0, keepdims=True)
    i1 = jnp.min(jnp.where(m0 == v1, eiota, N_EXPERTS), axis=0, keepdims=True)
    m1 = jnp.where(eiota == i1, neg, m0)
    v2 = jnp.max(m1, axis=0, keepdims=True)
    i2 = jnp.min(jnp.where(m1 == v2, eiota, N_EXPERTS), axis=0, keepdims=True)
    e2 = jnp.exp(v2 - v1)
    w1 = p_grp / (1.0 + e2)
    w2 = p_grp * e2 / (1.0 + e2)

    oh1 = eiota == i1
    oh2 = eiota == i2
    used = jnp.where(oh1, 1.0, jnp.where(oh2, 1.0, 0.0))
    in_strip = _dot(used.astype(BF16), before_ref[...])
    pieces = jnp.floor((jnp.sum(used, axis=1, keepdims=True) + (PIECE - 1.0)) * (1.0 / PIECE))
    er = lax.broadcasted_iota(jnp.int32, (N_EXPERTS, N_EXPERTS), 0)
    ec = lax.broadcasted_iota(jnp.int32, (N_EXPERTS, N_EXPERTS), 1)
    earlier = jnp.where(ec < er, 1.0, 0.0)
    strip_start = _dot(earlier, jnp.broadcast_to(pieces * PIECE, (N_EXPERTS, LANES)))[:, 0:1]
    row = strip_start + in_strip
    r1 = jnp.sum(jnp.where(oh1, row, 0.0), axis=0, keepdims=True)
    r2 = jnp.sum(jnp.where(oh2, row, 0.0), axis=0, keepdims=True)

    lrow = jnp.concatenate([r1, r2], axis=0).astype(jnp.int32)
    lrow_ref[...] = lrow
    gw_ref[...] = jnp.concatenate([w1, w2], axis=0)
    np_ref[...] = jnp.broadcast_to(pieces, np_ref.shape).astype(jnp.int32)

    riota = lax.broadcasted_iota(jnp.int32, (hs_ref.shape[0], ts), 0)
    pick = jnp.where(riota == lrow[0:1, :], 1.0, jnp.where(riota == lrow[1:2, :], 1.0, 0.0))
    hs_ref[...] = _dot(pick.astype(BF16), h2.astype(BF16))


def _const_spec(shape):
    return pl.BlockSpec(shape, lambda *_: (0,) * len(shape), pipeline_mode=pl.Buffered(1))


def _mixer(x, g1, w_in, conv_a_w, w_out_a, conv_b_w, conv_b_b, ln_g, ln_b, w_out_b, b_gate, w_o,
           g2, w_r, b_r):
    bsz, seq, d = x.shape
    wa = w_out_a.shape[0]
    wb = w_out_b.shape[0]
    ts = SEQ_TILE
    n_s = seq // ts
    t = bsz * seq
    taps_b = conv_b_w.astype(BF16).reshape(conv_b_w.shape[0], wb // LANES, 1, LANES)
    taps_b = jnp.broadcast_to(taps_b, taps_b.shape[:2] + (2 * SUBLANES, LANES))
    before = jnp.triu(jnp.ones((ts, ts), BF16), k=1)
    consts = (g1, w_in, conv_a_w, w_out_a, taps_b, conv_b_b, ln_g, ln_b, w_out_b, b_gate, w_o,
              g2, w_r, b_r, before)
    tok_spec = pl.BlockSpec((TOP_K, ts), lambda b, s: (0, b * n_s + s))
    return pl.pallas_call(
        functools.partial(_mixer_kernel, width_a=conv_a_w.shape[0], width_b=conv_b_w.shape[0],
                          d_model=d, wa=wa, wb=wb),
        grid=(bsz, n_s),
        in_specs=[pl.BlockSpec((None, ts, d), lambda b, s: (b, s, 0))]
                 + [_const_spec(c.shape) for c in consts],
        out_specs=[pl.BlockSpec((None, ts, d), lambda b, s: (b, s, 0)),
                   tok_spec, tok_spec,
                   pl.BlockSpec((None, N_EXPERTS, LANES), lambda b, s: (b * n_s + s, 0, 0)),
                   pl.BlockSpec((SORTED_ROWS, d), lambda b, s: (b * n_s + s, 0))],
        out_shape=[jax.ShapeDtypeStruct((bsz, seq, d), F32),
                   jax.ShapeDtypeStruct((TOP_K, t), jnp.int32),
                   jax.ShapeDtypeStruct((TOP_K, t), F32),
                   jax.ShapeDtypeStruct((t // ts, N_EXPERTS, LANES), jnp.int32),
                   jax.ShapeDtypeStruct((t // ts * SORTED_ROWS, d), F32)],
        scratch_shapes=[pltpu.VMEM((wa // LANES, HALO_A + ts, LANES), F32),
                        pltpu.VMEM((wb // LANES, HALO_B + ts + SUBLANES, LANES), F32),
                        pltpu.VMEM((2, wb // LANES, (HALO_B + ts) // 2, LANES), jnp.uint32),
                        pltpu.VMEM((ts, wb), F32),
                        pltpu.VMEM((ts, 3 * wa + 2 * d), F32),
                        pltpu.VMEM((ts, wa), BF16),
                        pltpu.VMEM((ts, wb), BF16)],
        compiler_params=pltpu.CompilerParams(
            dimension_semantics=("arbitrary", "arbitrary"), vmem_limit_bytes=MIXER_VMEM_LIMIT),
        name="mixer",
    )(x, *consts)


def _zero_unwritten(used_ref, ys_ref, zero, sem):
    n_tiles = used_ref.shape[0]
    rb = zero.shape[0]

    def fill(row, n_rows):
        return pltpu.make_async_copy(zero.at[pl.ds(0, n_rows), :],
                                     ys_ref.at[pl.ds(row, n_rows), :], sem)

    dump = [fill(r, rb) for r in range(n_tiles * SORTED_ROWS, ys_ref.shape[0], rb)]

    def tails(wait):
        def body(i, carry):
            used = used_ref[i]
            tail = SORTED_ROWS - used
            row = i * SORTED_ROWS + used
            size = SORTED_ROWS - TOP_K * SEQ_TILE
            while size >= PIECE:
                @pl.when((tail & size) != 0)
                def _(row=row, size=size):
                    c = fill(pl.multiple_of(row, PIECE), size)
                    c.wait() if wait else c.start()
                row = row + (tail & size)
                size //= 2
            return carry
        lax.fori_loop(0, n_tiles, body, 0)

    zero[...] = jnp.zeros(zero.shape, F32)
    for c in dump:
        c.start()
    tails(False)
    for c in dump:
        c.wait()
    tails(True)


def _expert_kernel(bexp_ref, nused_ref, src_ref, dst_ref, used_ref, hs_ref, wg_ref, wu_ref, wd_ref,
                   ys_ref, xbuf, ybuf, wg_bf, wu_bf, wd_bf, sem_in, sem_out):
    b = pl.program_id(0)
    n_used = nused_ref[0]
    rb = xbuf.shape[1]

    @pl.when(b == 0)
    def _():
        _zero_unwritten(used_ref, ys_ref, ybuf.at[1], sem_out.at[1])

    def piece_copies(table_ref, blk, hbm_ref, buf, slot, sem, inbound):
        copies = []
        for j in range(PIECES_PER_BLOCK):
            row = pl.multiple_of(table_ref[blk * PIECES_PER_BLOCK + j] * PIECE, PIECE)
            hbm = hbm_ref.at[pl.ds(row, PIECE), :]
            vmem = buf.at[slot, pl.ds(j * PIECE, PIECE), :]
            copies.append(pltpu.make_async_copy(hbm, vmem, sem.at[slot]) if inbound
                          else pltpu.make_async_copy(vmem, hbm, sem.at[slot]))
        return copies

    def start_in(blk, slot):
        for c in piece_copies(src_ref, blk, hs_ref, xbuf, slot, sem_in, True):
            c.start()

    def start_out(blk, slot):
        for c in piece_copies(dst_ref, blk, ys_ref, ybuf, slot, sem_out, False):
            c.start()

    def wait_in(slot):
        pltpu.make_async_copy(hs_ref.at[pl.ds(0, rb), :], xbuf.at[slot], sem_in.at[slot]).wait()

    def wait_out(slot):
        pltpu.make_async_copy(ybuf.at[slot], ys_ref.at[pl.ds(0, rb), :], sem_out.at[slot]).wait()

    @pl.when(b < n_used)
    def _():
        slot = lax.rem(b, 2)
        other = 1 - slot
        prev = jnp.maximum(b - 1, 0)

        @pl.when(b == 0)
        def _():
            start_in(0, 0)

        @pl.when(b + 1 < n_used)
        def _():
            start_in(b + 1, other)

        @pl.when((b == 0) | (bexp_ref[prev] != bexp_ref[b]))
        def _():
            wg_bf[...] = wg_ref[...].astype(BF16)
            wu_bf[...] = wu_ref[...].astype(BF16)
            wd_bf[...] = wd_ref[...].astype(BF16)

        wait_in(slot)

        @pl.when(b >= 2)
        def _():
            wait_out(slot)

        xb = xbuf[slot].astype(BF16)
        a = _dot(xb, wg_bf[...])
        hid = (a * _sigmoid(a) * _dot(xb, wu_bf[...])).astype(BF16)
        ybuf[slot] = _dot(hid, wd_bf[...])
        start_out(b, slot)

        @pl.when(b == n_used - 1)
        def _():
            @pl.when(b >= 1)
            def _():
                wait_out(other)
            wait_out(slot)


def _experts(hs, w_gate, w_up, w_down, bexp, nused, src, dst, used_rows, n_out_rows):
    _, d = hs.shape
    _, _, de = w_gate.shape
    rb = EXPERT_TILE
    assert (n_out_rows - hs.shape[0]) % rb == 0
    wmap = lambda b, bexp, nused, src, dst, used: (bexp[b], 0, 0)
    grid_spec = pltpu.PrefetchScalarGridSpec(
        num_scalar_prefetch=5,
        grid=(bexp.shape[0],),
        in_specs=[pl.BlockSpec(memory_space=pl.ANY),
                  pl.BlockSpec((None, d, de), wmap),
                  pl.BlockSpec((None, d, de), wmap),
                  pl.BlockSpec((None, de, d), wmap)],
        out_specs=pl.BlockSpec(memory_space=pl.ANY),
        scratch_shapes=[pltpu.VMEM((2, rb, d), F32), pltpu.VMEM((2, rb, d), F32),
                        pltpu.VMEM((d, de), BF16), pltpu.VMEM((d, de), BF16),
                        pltpu.VMEM((de, d), BF16),
                        pltpu.SemaphoreType.DMA((2,)), pltpu.SemaphoreType.DMA((2,))],
    )
    return pl.pallas_call(
        _expert_kernel,
        grid_spec=grid_spec,
        out_shape=jax.ShapeDtypeStruct((n_out_rows, d), F32),
        compiler_params=pltpu.CompilerParams(dimension_semantics=("arbitrary",),
                                             vmem_limit_bytes=VMEM_LIMIT),
        name="experts",
    )(bexp, nused, src, dst, used_rows, hs, w_gate, w_up, w_down)


def _combine_kernel(x1_ref, p_ref, lrow_ref, gw_ref, g3_ref, wpg_ref, wpp_ref, gf_ref, ys_ref,
                    out_ref):
    ts = x1_ref.shape[0]
    riota = lax.broadcasted_iota(jnp.int32, (ys_ref.shape[0], ts), 0)
    pick = jnp.where(riota == lrow_ref[0:1, :], gw_ref[0:1, :],
                     jnp.where(riota == lrow_ref[1:2, :], gw_ref[1:2, :], 0.0))
    moe = lax.dot_general(pick.astype(BF16), ys_ref[...].astype(BF16), (((0,), (0,)), ((), ())),
                          preferred_element_type=F32)
    x2 = x1_ref[...] + moe
    hp = _rms(x2, g3_ref[...]).astype(BF16)
    proj = _dot(p_ref[...].astype(BF16), wpp_ref[...])
    x3 = x2 + _sigmoid(_dot(hp, wpg_ref[...])) * proj
    out_ref[...] = _rms(x3, gf_ref[...])


def _combine(x1, p, lrow, gw, ys, g3, w_pg, w_pp, gf):
    t, d = x1.shape
    pd = p.shape[1]
    ts = SEQ_TILE
    tile = lambda i: (i, 0)
    const = lambda i: (0, 0)
    return pl.pallas_call(
        _combine_kernel,
        grid=(t // ts,),
        in_specs=[pl.BlockSpec((ts, d), tile),
                  pl.BlockSpec((ts, pd), tile),
                  pl.BlockSpec((TOP_K, ts), lambda i: (0, i)),
                  pl.BlockSpec((TOP_K, ts), lambda i: (0, i)),
                  pl.BlockSpec((1, d), const),
                  pl.BlockSpec((d, d), const),
                  pl.BlockSpec((pd, d), const),
                  pl.BlockSpec((1, d), const),
                  pl.BlockSpec((SORTED_ROWS, d), tile)],
        out_specs=pl.BlockSpec((ts, d), tile),
        out_shape=jax.ShapeDtypeStruct((t, d), F32),
        compiler_params=pltpu.CompilerParams(dimension_semantics=("arbitrary",),
                                             vmem_limit_bytes=VMEM_LIMIT),
        name="combine",
    )(x1, p, lrow, gw, g3, w_pg, w_pp, gf, ys)


def _piece_plan(n_pieces):
    n_tiles = n_pieces.shape[0]
    ppb = PIECES_PER_BLOCK
    tile_pieces = SORTED_ROWS // PIECE
    i32 = jnp.int32
    excl = lambda v, axis: jnp.cumsum(v, axis=axis, dtype=i32) - v
    base = (jnp.arange(n_tiles, dtype=i32) * tile_pieces)[:, None] + excl(n_pieces, 1)
    per_expert = n_pieces.T
    blocks = (jnp.sum(per_expert, axis=1, dtype=i32) + ppb - 1) // ppb
    first_block = excl(blocks, 0)
    n_used = jnp.sum(blocks, dtype=i32)

    max_blocks = (TOP_K * n_tiles * SEQ_TILE // PIECE + n_tiles * N_EXPERTS) // ppb + N_EXPERTS
    block_ids = jnp.arange(max_blocks, dtype=i32)
    blk = jnp.minimum(block_ids, n_used - 1)
    bexp = jnp.sum((first_block[None, :] <= blk[:, None]).astype(i32), axis=1) - 1
    strip_lo = jnp.take(excl(per_expert, 1), bexp, axis=0)[:, None, :]
    strip_len = jnp.take(per_expert, bexp, axis=0)[:, None, :]
    strip_base = jnp.take(base.T, bexp, axis=0)[:, None, :]
    local = ((block_ids - jnp.take(first_block, bexp)) * ppb)[:, None] + jnp.arange(ppb, dtype=i32)
    local = local[:, :, None]
    inside = (local >= strip_lo) & (local < strip_lo + strip_len)
    src = jnp.sum(jnp.where(inside, strip_base + local - strip_lo, 0), axis=2, dtype=i32).reshape(-1)
    real = jnp.any(inside, axis=2).reshape(-1)
    n_dump = 2 * ppb
    dump = (block_ids % 2)[:, None] * ppb + jnp.arange(ppb, dtype=i32)[None, :]
    dst = jnp.where(real, src, n_tiles * tile_pieces + dump.reshape(-1))
    n_out_rows = (n_tiles * tile_pieces + n_dump) * PIECE
    return bexp, n_used.reshape(1), src, dst, n_out_rows


def _layer(x, p, mix_norm_g, w_in, conv_a_w, w_out_a, conv_b_w, conv_b_b, ln_b_g, ln_b_b, w_out_b,
           b_gate, w_o, ffn_norm_g, w_rg, b_rg, w_re, b_re, w_eg, w_eu, w_ed, ple_norm_g, w_pg, w_pp,
           out_norm_g):
    bsz, seq, d = x.shape
    t = bsz * seq
    row = lambda v: v.reshape(1, -1).astype(F32)
    w_r = jnp.zeros((ROUTER_ROWS, d), F32).at[0:N_GROUPS].set(w_rg.T).at[8:].set(w_re.T)
    b_r = jnp.zeros((ROUTER_ROWS,), F32).at[0:N_GROUPS].set(b_rg).at[8:].set(b_re)
    b_r = jnp.broadcast_to(b_r[:, None], (ROUTER_ROWS, LANES))

    x1, lrow, gw, n_pieces, hs = _mixer(
        x, row(mix_norm_g), w_in.astype(BF16), conv_a_w, w_out_a.astype(BF16), conv_b_w,
        row(conv_b_b), row(ln_b_g), row(ln_b_b), w_out_b.astype(BF16), row(b_gate),
        w_o.astype(BF16), row(ffn_norm_g), w_r, b_r)
    x1 = x1.reshape(t, d)

    n_pieces = n_pieces[:, :, 0]
    bexp, n_used, src, dst, n_out_rows = _piece_plan(n_pieces)
    used_rows = jnp.sum(n_pieces, axis=1, dtype=jnp.int32) * PIECE
    ys = _experts(hs, w_eg, w_eu, w_ed, bexp, n_used, src, dst, used_rows, n_out_rows)
    out = _combine(x1, p.reshape(t, -1), lrow, gw, ys, row(ple_norm_g),
                   w_pg.astype(BF16), w_pp.astype(BF16), row(out_norm_g))
    return out.reshape(bsz, seq, d)


def kernel(x, p, mix_norm_g, w_in, conv_a_w, w_out_a, conv_b_w, conv_b_b, ln_b_g, ln_b_b, w_out_b, b_gate, w_o, ffn_norm_g, w_router_group, b_router_group, w_router_expert, b_router_expert, w_exp_gate, w_exp_up, w_exp_down, ple_norm_g, w_ple_gate, w_ple_proj, final_norm_g):
    assert w_in.shape[0] == 1, "the combine kernel fuses the final RMSNorm: single layer only"
    return _layer(x, p[0], mix_norm_g[0], w_in[0], conv_a_w[0], w_out_a[0], conv_b_w[0],
                  conv_b_b[0], ln_b_g[0], ln_b_b[0], w_out_b[0], b_gate[0], w_o[0], ffn_norm_g[0],
                  w_router_group[0], b_router_group[0], w_router_expert[0], b_router_expert[0],
                  w_exp_gate[0], w_exp_up[0], w_exp_down[0], ple_norm_g[0], w_ple_gate[0],
                  w_ple_proj[0], final_norm_g)
```

```python
import functools

import jax
import jax.numpy as jnp
from jax import lax
from jax.experimental import pallas as pl
from jax.experimental.pallas import tpu as pltpu

F32 = jnp.float32
BF16 = jnp.bfloat16
EPS = 1e-6

LANES = 128
SUBLANES = 8
N_GROUPS = 4
EXPERTS_PER_GROUP = 8
N_EXPERTS = N_GROUPS * EXPERTS_PER_GROUP
TOP_K = 2
ROUTER_ROWS = 8 + N_EXPERTS

SEQ_TILE = 512
CONV_ROWS = 128
MXU_COLS = 256
HALO_A = 8
HALO_B = 32
TAIL_ROWS = 128
COMBINE_ROWS = 256
PIECE = SUBLANES
EXPERT_TILE = 1024
PIECES_PER_BLOCK = EXPERT_TILE // PIECE
SORTED_ROWS = TOP_K * SEQ_TILE + N_EXPERTS * PIECE
VMEM_LIMIT = 56 * 1024 * 1024
MIXER_VMEM_LIMIT = 60 * 1024 * 1024


def _sigmoid(v):
    return 1.0 / (1.0 + jnp.exp(-v))


def _rms(v, g):
    return v * lax.rsqrt(jnp.mean(v * v, axis=-1, keepdims=True) + EPS) * g


def _dot(a, b):
    return jnp.dot(a, b, preferred_element_type=F32)


def _conv_steps(buf, w_ref, width, halo, rows, emit):
    steps = []
    for j in range(buf.shape[0]):
        for base in range(0, rows, CONV_ROWS):
            def step(j=j, base=base):
                lanes = slice(j * LANES, (j + 1) * LANES)
                acc = None
                for k in range(width):
                    start = base + halo - (width - 1) + k
                    term = w_ref[k:k + 1, lanes] * buf[j, start:start + CONV_ROWS, :]
                    acc = term if acc is None else acc + term
                emit(slice(base, base + CONV_ROWS), lanes, acc)
            steps.append((width * CONV_ROWS // 8, step))
    return steps


def _packed_conv_steps(packed, w_ref, width, halo, rows, emit):
    group = 2 * SUBLANES
    steps = []
    for j in range(packed.shape[1]):
        for base in range(0, rows, CONV_ROWS):
            def step(j=j, base=base):
                accs = [None] * (CONV_ROWS // group)
                for k in range(width):
                    shift = k % 2
                    tap = w_ref[k, j]
                    for g in range(len(accs)):
                        word = (base + g * group + halo - (width - 1) + k - shift) // 2
                        vals = pltpu.bitcast(packed[shift, j, word:word + SUBLANES, :], BF16)
                        accs[g] = vals * tap if accs[g] is None else accs[g] + vals * tap
                for g, acc in enumerate(accs):
                    lo = base + g * group
                    emit(slice(lo, lo + group), slice(j * LANES, (j + 1) * LANES), acc.astype(F32))
            steps.append((width * CONV_ROWS // 16, step))
    return steps


def _interleave(mxu_steps, vpu_steps):
    total_m = sum(c for c, _ in mxu_steps)
    total_v = sum(c for c, _ in vpu_steps)
    done_m = 0
    done_v = 0
    pending = list(vpu_steps)
    for cost, step in mxu_steps:
        step()
        done_m += cost
        while pending and done_v * total_m < done_m * total_v:
            c, vstep = pending.pop(0)
            vstep()
            done_v += c
    for _, vstep in pending:
        vstep()


def _mixer_kernel(x_ref, g1_ref, win_ref, caw_ref, woa_ref, cbw_ref, cbb_ref, lng_ref, lnb_ref,
                  wob_ref, bg_ref, wo_ref, g2_ref, wr_ref, br_ref, before_ref,
                  x1_ref, lrow_ref, gw_ref, np_ref, hs_ref,
                  abuf, bbuf, bpk, cbuf, pbuf, zbuf, swbuf, *, width_a, width_b, d_model, wa, wb):
    ts = x_ref.shape[0]

    @pl.when(pl.program_id(1) == 0)
    def _():
        abuf[:, 0:HALO_A, :] = jnp.zeros((abuf.shape[0], HALO_A, LANES), F32)
        bbuf[:, 0:HALO_B, :] = jnp.zeros((bbuf.shape[0], HALO_B, LANES), F32)
        bbuf[:, HALO_B + ts:, :] = jnp.zeros((bbuf.shape[0], SUBLANES, LANES), F32)

    h = _rms(x_ref[...], g1_ref[...]).astype(BF16)
    col_b, col_c, col_x = 0, wa, 2 * wa
    col_v, col_g, col_gate = 3 * wa, 3 * wa + wb, 3 * wa + 2 * wb

    def proj(col, q):
        return _dot(h, win_ref[:, col + q * MXU_COLS:col + (q + 1) * MXU_COLS])

    raw_cols = ((col_c, wa), (col_x, wa), (col_b, wa), (col_gate, 2 * d_model))
    pb_c, pb_x, pb_b, pb_g = 0, wa, 2 * wa, 3 * wa
    raw_steps = []
    off = 0
    for col, width in raw_cols:
        for q in range(width // MXU_COLS):
            def raw_step(col=col, q=q, off=off):
                lo = off + q * MXU_COLS
                pbuf[:, lo:lo + MXU_COLS] = proj(col, q)
            raw_steps.append((1, raw_step))
        off += width

    def emit_b(rows, lanes, acc):
        cbuf[rows, lanes] = acc

    conv_steps = _packed_conv_steps(bpk, cbw_ref, width_b, HALO_B, ts, emit_b)

    n_glu = wb // MXU_COLS
    lanes_per_glu = MXU_COLS // LANES
    for q in range(n_glu):
        u = proj(col_v, q) * _sigmoid(proj(col_g, q))
        for jj in range(lanes_per_glu):
            j = q * lanes_per_glu + jj
            bbuf[j, HALO_B:HALO_B + ts, :] = u[:, jj * LANES:(jj + 1) * LANES]
            for shift in range(2):
                rows_f32 = bbuf[j, shift:shift + HALO_B + ts, :]
                bpk[shift, j] = pltpu.bitcast(rows_f32.astype(BF16), jnp.uint32)
            bbuf[j, 0:HALO_B, :] = bbuf[j, ts:ts + HALO_B, :]
        _interleave(raw_steps[q * len(raw_steps) // n_glu:(q + 1) * len(raw_steps) // n_glu],
                    conv_steps[q * len(conv_steps) // n_glu:(q + 1) * len(conv_steps) // n_glu])

    for j in range(wa // LANES):
        abuf[j, HALO_A:HALO_A + ts, :] = (pbuf[:, pb_c + j * LANES:pb_c + (j + 1) * LANES]
                                           * pbuf[:, pb_x + j * LANES:pb_x + (j + 1) * LANES])

    def emit_a(rows, lanes, acc):
        zbuf[rows, lanes] = (pbuf[rows, pb_b + lanes.start:pb_b + lanes.stop] * acc).astype(BF16)

    for _, step in _conv_steps(abuf, caw_ref, width_a, HALO_A, ts, emit_a):
        step()
    abuf[:, 0:HALO_A, :] = abuf[:, ts:ts + HALO_A, :]

    for r in range(0, ts, TAIL_ROWS):
        rows = slice(r, r + TAIL_ROWS)
        c = cbuf[rows, :] + cbb_ref[...]
        mu = jnp.mean(c, axis=-1, keepdims=True)
        cc = c - mu
        var = jnp.mean(cc * cc, axis=-1, keepdims=True)
        ln = cc * lax.rsqrt(var + EPS) * lng_ref[...] + lnb_ref[...]
        swbuf[rows, :] = (ln * _sigmoid(ln)).astype(BF16)
        gcols = slice(pb_g, pb_g + 2 * d_model)
        pbuf[rows, gcols] = _sigmoid(pbuf[rows, gcols] + bg_ref[...])
    mix = (pbuf[:, pb_g:pb_g + d_model] * _dot(zbuf[...], woa_ref[...])
           + pbuf[:, pb_g + d_model:pb_g + 2 * d_model] * _dot(swbuf[...], wob_ref[...]))

    x1 = x_ref[...] + _dot(mix.astype(BF16), wo_ref[...])
    x1_ref[...] = x1

    h2 = _rms(x1, g2_ref[...])
    logits = lax.dot_general(wr_ref[...], h2, (((1,), (1,)), ((), ())),
                             preferred_element_type=F32) + br_ref[:, 0:1]
    gl = logits[0:N_GROUPS, :]
    gmax = jnp.max(gl, axis=0, keepdims=True)
    giota = lax.broadcasted_iota(jnp.int32, gl.shape, 0)
    grp = jnp.min(jnp.where(gl == gmax, giota, N_GROUPS), axis=0, keepdims=True)
    p_grp = 1.0 / jnp.sum(jnp.exp(gl - gmax), axis=0, keepdims=True)

    el = logits[8:8 + N_EXPERTS, :]
    eiota = lax.broadcasted_iota(jnp.int32, el.shape, 0)
    neg = jnp.float32(-jnp.inf)
    m0 = jnp.where(lax.div(eiota, EXPERTS_PER_GROUP) == grp, el, neg)
    v1 = jnp.max(m0, axis=0, keepdims=True)
    i1 = jnp.min(jnp.where(m0 == v1, eiota, N_EXPERTS), axis=0, keepdims=True)
    m1 = jnp.where(eiota == i1, neg, m0)
    v2 = jnp.max(m1, axis=0, keepdims=True)
    i2 = jnp.min(jnp.where(m1 == v2, eiota, N_EXPERTS), axis=0, keepdims=True)
    e2 = jnp.exp(v2 - v1)
    w1 = p_grp / (1.0 + e2)
    w2 = p_grp * e2 / (1.0 + e2)

    oh1 = eiota == i1
    oh2 = eiota == i2
    used = jnp.where(oh1, 1.0, jnp.where(oh2, 1.0, 0.0))
    in_strip = _dot(used.astype(BF16), before_ref[...])
    pieces = jnp.floor((jnp.sum(used, axis=1, keepdims=True) + (PIECE - 1.0)) * (1.0 / PIECE))
    er = lax.broadcasted_iota(jnp.int32, (N_EXPERTS, N_EXPERTS), 0)
    ec = lax.broadcasted_iota(jnp.int32, (N_EXPERTS, N_EXPERTS), 1)
    earlier = jnp.where(ec < er, 1.0, 0.0)
    strip_start = _dot(earlier, jnp.broadcast_to(pieces * PIECE, (N_EXPERTS, LANES)))[:, 0:1]
    row = strip_start + in_strip
    r1 = jnp.sum(jnp.where(oh1, row, 0.0), axis=0, keepdims=True)
    r2 = jnp.sum(jnp.where(oh2, row, 0.0), axis=0, keepdims=True)

    lrow = jnp.concatenate([r1, r2], axis=0).astype(jnp.int32)
    lrow_ref[...] = lrow
    gw_ref[...] = jnp.concatenate([w1, w2], axis=0)
    np_ref[...] = jnp.broadcast_to(pieces, np_ref.shape).astype(jnp.int32)

    riota = lax.broadcasted_iota(jnp.int32, (hs_ref.shape[0], ts), 0)
    pick = jnp.where(riota == lrow[0:1, :], 1.0, jnp.where(riota == lrow[1:2, :], 1.0, 0.0))
    hs_ref[...] = _dot(pick.astype(BF16), h2.astype(BF16))


def _const_spec(shape):
    return pl.BlockSpec(shape, lambda *_: (0,) * len(shape), pipeline_mode=pl.Buffered(1))


def _mixer(x, g1, w_in, conv_a_w, w_out_a, conv_b_w, conv_b_b, ln_g, ln_b, w_out_b, b_gate, w_o,
           g2, w_r, b_r):
    bsz, seq, d = x.shape
    wa = w_out_a.shape[0]
    wb = w_out_b.shape[0]
    ts = SEQ_TILE
    n_s = seq // ts
    t = bsz * seq
    taps_b = conv_b_w.astype(BF16).reshape(conv_b_w.shape[0], wb // LANES, 1, LANES)
    taps_b = jnp.broadcast_to(taps_b, taps_b.shape[:2] + (2 * SUBLANES, LANES))
    before = jnp.triu(jnp.ones((ts, ts), BF16), k=1)
    consts = (g1, w_in, conv_a_w, w_out_a, taps_b, conv_b_b, ln_g, ln_b, w_out_b, b_gate, w_o,
              g2, w_r, b_r, before)
    tok_spec = pl.BlockSpec((TOP_K, ts), lambda b, s: (0, b * n_s + s))
    return pl.pallas_call(
        functools.partial(_mixer_kernel, width_a=conv_a_w.shape[0], width_b=conv_b_w.shape[0],
                          d_model=d, wa=wa, wb=wb),
        grid=(bsz, n_s),
        in_specs=[pl.BlockSpec((None, ts, d), lambda b, s: (b, s, 0))]
                 + [_const_spec(c.shape) for c in consts],
        out_specs=[pl.BlockSpec((None, ts, d), lambda b, s: (b, s, 0)),
                   tok_spec, tok_spec,
                   pl.BlockSpec((None, N_EXPERTS, LANES), lambda b, s: (b * n_s + s, 0, 0)),
                   pl.BlockSpec((SORTED_ROWS, d), lambda b, s: (b * n_s + s, 0))],
        out_shape=[jax.ShapeDtypeStruct((bsz, seq, d), F32),
                   jax.ShapeDtypeStruct((TOP_K, t), jnp.int32),
                   jax.ShapeDtypeStruct((TOP_K, t), F32),
                   jax.ShapeDtypeStruct((t // ts, N_EXPERTS, LANES), jnp.int32),
                   jax.ShapeDtypeStruct((t // ts * SORTED_ROWS, d), F32)],
        scratch_shapes=[pltpu.VMEM((wa // LANES, HALO_A + ts, LANES), F32),
                        pltpu.VMEM((wb // LANES, HALO_B + ts + SUBLANES, LANES), F32),
                        pltpu.VMEM((2, wb // LANES, (HALO_B + ts) // 2, LANES), jnp.uint32),
                        pltpu.VMEM((ts, wb), F32),
                        pltpu.VMEM((ts, 3 * wa + 2 * d), F32),
                        pltpu.VMEM((ts, wa), BF16),
                        pltpu.VMEM((ts, wb), BF16)],
        compiler_params=pltpu.CompilerParams(
            dimension_semantics=("arbitrary", "arbitrary"), vmem_limit_bytes=MIXER_VMEM_LIMIT),
        name="mixer",
    )(x, *consts)


def _zero_unwritten(used_ref, ys_ref, zero, sem):
    n_tiles = used_ref.shape[0]
    rb = zero.shape[0]

    def fill(row, n_rows):
        return pltpu.make_async_copy(zero.at[pl.ds(0, n_rows), :],
                                     ys_ref.at[pl.ds(row, n_rows), :], sem)

    dump = [fill(r, rb) for r in range(n_tiles * SORTED_ROWS, ys_ref.shape[0], rb)]

    def tails(wait):
        def body(i, carry):
            used = used_ref[i]
            tail = SORTED_ROWS - used
            row = i * SORTED_ROWS + used
            size = SORTED_ROWS - TOP_K * SEQ_TILE
            while size >= PIECE:
                @pl.when((tail & size) != 0)
                def _(row=row, size=size):
                    c = fill(pl.multiple_of(row, PIECE), size)
                    c.wait() if wait else c.start()
                row = row + (tail & size)
                size //= 2
            return carry
        lax.fori_loop(0, n_tiles, body, 0)

    zero[...] = jnp.zeros(zero.shape, F32)
    for c in dump:
        c.start()
    tails(False)
    for c in dump:
        c.wait()
    tails(True)


def _expert_kernel(bexp_ref, nused_ref, src_ref, dst_ref, used_ref, hs_ref, wg_ref, wu_ref, wd_ref,
                   ys_ref, xbuf, ybuf, wg_bf, wu_bf, wd_bf, sem_in, sem_out):
    b = pl.program_id(0)
    n_used = nused_ref[0]
    rb = xbuf.shape[1]

    @pl.when(b == 0)
    def _():
        _zero_unwritten(used_ref, ys_ref, ybuf.at[1], sem_out.at[1])

    def piece_copies(table_ref, blk, hbm_ref, buf, slot, sem, inbound):
        copies = []
        for j in range(PIECES_PER_BLOCK):
            row = pl.multiple_of(table_ref[blk * PIECES_PER_BLOCK + j] * PIECE, PIECE)
            hbm = hbm_ref.at[pl.ds(row, PIECE), :]
            vmem = buf.at[slot, pl.ds(j * PIECE, PIECE), :]
            copies.append(pltpu.make_async_copy(hbm, vmem, sem.at[slot]) if inbound
                          else pltpu.make_async_copy(vmem, hbm, sem.at[slot]))
        return copies

    def start_in(blk, slot):
        for c in piece_copies(src_ref, blk, hs_ref, xbuf, slot, sem_in, True):
            c.start()

    def start_out(blk, slot):
        for c in piece_copies(dst_ref, blk, ys_ref, ybuf, slot, sem_out, False):
            c.start()

    def wait_in(slot):
        pltpu.make_async_copy(hs_ref.at[pl.ds(0, rb), :], xbuf.at[slot], sem_in.at[slot]).wait()

    def wait_out(slot):
        pltpu.make_async_copy(ybuf.at[slot], ys_ref.at[pl.ds(0, rb), :], sem_out.at[slot]).wait()

    @pl.when(b < n_used)
    def _():
        slot = lax.rem(b, 2)
        other = 1 - slot
        prev = jnp.maximum(b - 1, 0)

        @pl.when(b == 0)
        def _():
            start_in(0, 0)

        @pl.when(b + 1 < n_used)
        def _():
            start_in(b + 1, other)

        @pl.when((b == 0) | (bexp_ref[prev] != bexp_ref[b]))
        def _():
            wg_bf[...] = wg_ref[...].astype(BF16)
            wu_bf[...] = wu_ref[...].astype(BF16)
            wd_bf[...] = wd_ref[...].astype(BF16)

        wait_in(slot)

        @pl.when(b >= 2)
        def _():
            wait_out(slot)

        xb = xbuf[slot].astype(BF16)
        a = _dot(xb, wg_bf[...])
        hid = (a * _sigmoid(a) * _dot(xb, wu_bf[...])).astype(BF16)
        ybuf[slot] = _dot(hid, wd_bf[...])
        start_out(b, slot)

        @pl.when(b == n_used - 1)
        def _():
            @pl.when(b >= 1)
            def _():
                wait_out(other)
            wait_out(slot)


def _experts(hs, w_gate, w_up, w_down, bexp, nused, src, dst, used_rows, n_out_rows):
    _, d = hs.shape
    _, _, de = w_gate.shape
    rb = EXPERT_TILE
    assert (n_out_rows - hs.shape[0]) % rb == 0
    wmap = lambda b, bexp, nused, src, dst, used: (bexp[b], 0, 0)
    grid_spec = pltpu.PrefetchScalarGridSpec(
        num_scalar_prefetch=5,
        grid=(bexp.shape[0],),
        in_specs=[pl.BlockSpec(memory_space=pl.ANY),
                  pl.BlockSpec((None, d, de), wmap),
                  pl.BlockSpec((None, d, de), wmap),
                  pl.BlockSpec((None, de, d), wmap)],
        out_specs=pl.BlockSpec(memory_space=pl.ANY),
        scratch_shapes=[pltpu.VMEM((2, rb, d), F32), pltpu.VMEM((2, rb, d), F32),
                        pltpu.VMEM((d, de), BF16), pltpu.VMEM((d, de), BF16),
                        pltpu.VMEM((de, d), BF16),
                        pltpu.SemaphoreType.DMA((2,)), pltpu.SemaphoreType.DMA((2,))],
    )
    return pl.pallas_call(
        _expert_kernel,
        grid_spec=grid_spec,
        out_shape=jax.ShapeDtypeStruct((n_out_rows, d), F32),
        compiler_params=pltpu.CompilerParams(dimension_semantics=("arbitrary",),
                                             vmem_limit_bytes=VMEM_LIMIT),
        name="experts",
    )(bexp, nused, src, dst, used_rows, hs, w_gate, w_up, w_down)


def _combine_kernel(x1_ref, p_ref, lrow_ref, gw_ref, g3_ref, wpg_ref, wpp_ref, gf_ref, ys_ref,
                    out_ref):
    ts = x1_ref.shape[0]
    riota = lax.broadcasted_iota(jnp.int32, (ys_ref.shape[0], ts), 0)
    pick = jnp.where(riota == lrow_ref[0:1, :], gw_ref[0:1, :],
                     jnp.where(riota == lrow_ref[1:2, :], gw_ref[1:2, :], 0.0)).astype(BF16)
    ys = ys_ref[...].astype(BF16)
    for r in range(0, ts, COMBINE_ROWS):
        rows = slice(r, r + COMBINE_ROWS)
        moe = lax.dot_general(pick[:, rows], ys, (((0,), (0,)), ((), ())),
                              preferred_element_type=F32)
        x2 = x1_ref[rows, :] + moe
        hp = _rms(x2, g3_ref[...]).astype(BF16)
        proj = _dot(p_ref[rows, :].astype(BF16), wpp_ref[...])
        x3 = x2 + _sigmoid(_dot(hp, wpg_ref[...])) * proj
        out_ref[rows, :] = _rms(x3, gf_ref[...])


def _combine(x1, p, lrow, gw, ys, g3, w_pg, w_pp, gf):
    t, d = x1.shape
    pd = p.shape[1]
    ts = SEQ_TILE
    tile = lambda i: (i, 0)
    const = lambda i: (0, 0)
    return pl.pallas_call(
        _combine_kernel,
        grid=(t // ts,),
        in_specs=[pl.BlockSpec((ts, d), tile),
                  pl.BlockSpec((ts, pd), tile),
                  pl.BlockSpec((TOP_K, ts), lambda i: (0, i)),
                  pl.BlockSpec((TOP_K, ts), lambda i: (0, i)),
                  pl.BlockSpec((1, d), const),
                  pl.BlockSpec((d, d), const),
                  pl.BlockSpec((pd, d), const),
                  pl.BlockSpec((1, d), const),
                  pl.BlockSpec((SORTED_ROWS, d), tile)],
        out_specs=pl.BlockSpec((ts, d), tile),
        out_shape=jax.ShapeDtypeStruct((t, d), F32),
        compiler_params=pltpu.CompilerParams(dimension_semantics=("arbitrary",),
                                             vmem_limit_bytes=VMEM_LIMIT),
        name="combine",
    )(x1, p, lrow, gw, g3, w_pg, w_pp, gf, ys)


def _piece_plan(n_pieces):
    n_tiles = n_pieces.shape[0]
    ppb = PIECES_PER_BLOCK
    tile_pieces = SORTED_ROWS // PIECE
    i32 = jnp.int32
    excl = lambda v, axis: jnp.cumsum(v, axis=axis, dtype=i32) - v
    base = (jnp.arange(n_tiles, dtype=i32) * tile_pieces)[:, None] + excl(n_pieces, 1)
    per_expert = n_pieces.T
    blocks = (jnp.sum(per_expert, axis=1, dtype=i32) + ppb - 1) // ppb
    first_block = excl(blocks, 0)
    n_used = jnp.sum(blocks, dtype=i32)

    max_blocks = (TOP_K * n_tiles * SEQ_TILE // PIECE + n_tiles * N_EXPERTS) // ppb + N_EXPERTS
    block_ids = jnp.arange(max_blocks, dtype=i32)
    blk = jnp.minimum(block_ids, n_used - 1)
    bexp = jnp.sum((first_block[None, :] <= blk[:, None]).astype(i32), axis=1) - 1
    strip_lo = jnp.take(excl(per_expert, 1), bexp, axis=0)[:, None, :]
    strip_len = jnp.take(per_expert, bexp, axis=0)[:, None, :]
    strip_base = jnp.take(base.T, bexp, axis=0)[:, None, :]
    local = ((block_ids - jnp.take(first_block, bexp)) * ppb)[:, None] + jnp.arange(ppb, dtype=i32)
    local = local[:, :, None]
    inside = (local >= strip_lo) & (local < strip_lo + strip_len)
    src = jnp.sum(jnp.where(inside, strip_base + local - strip_lo, 0), axis=2, dtype=i32).reshape(-1)
    real = jnp.any(inside, axis=2).reshape(-1)
    n_dump = 2 * ppb
    dump = (block_ids % 2)[:, None] * ppb + jnp.arange(ppb, dtype=i32)[None, :]
    dst = jnp.where(real, src, n_tiles * tile_pieces + dump.reshape(-1))
    n_out_rows = (n_tiles * tile_pieces + n_dump) * PIECE
    return bexp, n_used.reshape(1), src, dst, n_out_rows


def _layer(x, p, mix_norm_g, w_in, conv_a_w, w_out_a, conv_b_w, conv_b_b, ln_b_g, ln_b_b, w_out_b,
           b_gate, w_o, ffn_norm_g, w_rg, b_rg, w_re, b_re, w_eg, w_eu, w_ed, ple_norm_g, w_pg, w_pp,
           out_norm_g):
    bsz, seq, d = x.shape
    t = bsz * seq
    row = lambda v: v.reshape(1, -1).astype(F32)
    w_r = jnp.zeros((ROUTER_ROWS, d), F32).at[0:N_GROUPS].set(w_rg.T).at[8:].set(w_re.T)
    b_r = jnp.zeros((ROUTER_ROWS,), F32).at[0:N_GROUPS].set(b_rg).at[8:].set(b_re)
    b_r = jnp.broadcast_to(b_r[:, None], (ROUTER_ROWS, LANES))

    x1, lrow, gw, n_pieces, hs = _mixer(
        x, row(mix_norm_g), w_in.astype(BF16), conv_a_w, w_out_a.astype(BF16), conv_b_w,
        row(conv_b_b), row(ln_b_g), row(ln_b_b), w_out_b.astype(BF16), row(b_gate),
        w_o.astype(BF16), row(ffn_norm_g), w_r, b_r)
    x1 = x1.reshape(t, d)

    n_pieces = n_pieces[:, :, 0]
    bexp, n_used, src, dst, n_out_rows = _piece_plan(n_pieces)
    used_rows = jnp.sum(n_pieces, axis=1, dtype=jnp.int32) * PIECE
    ys = _experts(hs, w_eg, w_eu, w_ed, bexp, n_used, src, dst, used_rows, n_out_rows)
    out = _combine(x1, p.reshape(t, -1), lrow, gw, ys, row(ple_norm_g),
                   w_pg.astype(BF16), w_pp.astype(BF16), row(out_norm_g))
    return out.reshape(bsz, seq, d)


def kernel(x, p, mix_norm_g, w_in, conv_a_w, w_out_a, conv_b_w, conv_b_b, ln_b_g, ln_b_b, w_out_b, b_gate, w_o, ffn_norm_g, w_router_group, b_router_group, w_router_expert, b_router_expert, w_exp_gate, w_exp_up, w_exp_down, ple_norm_g, w_ple_gate, w_ple_proj, final_norm_g):
    assert w_in.shape[0] == 1, "the combine kernel fuses the final RMSNorm: single layer only"
    return _layer(x, p[0], mix_norm_g[0], w_in[0], conv_a_w[0], w_out_a[0], conv_b_w[0],
                  conv_b_b[0], ln_b_g[0], ln_b_b[0], w_out_b[0], b_gate[0], w_o[0], ffn_norm_g[0],
                  w_router_group[0], b_router_group[0], w_router_expert[0], b_router_expert[0],
                  w_exp_gate[0], w_exp_up[0], w_exp_down[0], ple_norm_g[0], w_ple_gate[0],
                  w_ple_proj[0], final_norm_g)
```

```python
import functools

import jax
import jax.numpy as jnp
from jax import lax
from jax.experimental import pallas as pl
from jax.experimental.pallas import tpu as pltpu

F32 = jnp.float32
BF16 = jnp.bfloat16
EPS = 1e-6

LANES = 128
SUBLANES = 8
N_GROUPS = 4
EXPERTS_PER_GROUP = 8
N_EXPERTS = N_GROUPS * EXPERTS_PER_GROUP
TOP_K = 2
ROUTER_ROWS = 8 + N_EXPERTS

SEQ_TILE = 512
CONV_ROWS = 128
MXU_COLS = 256
RAW_SPLIT = (0, 1, 2, 5, 12)
HALO_A = 8
HALO_B = 32
TAIL_ROWS = 128
PIECE = SUBLANES
EXPERT_TILE = 512
PIECES_PER_BLOCK = EXPERT_TILE // PIECE
SORTED_ROWS = TOP_K * SEQ_TILE + N_EXPERTS * PIECE
VMEM_LIMIT = 56 * 1024 * 1024
MIXER_VMEM_LIMIT = 60 * 1024 * 1024


def _sigmoid(v):
    return 1.0 / (1.0 + jnp.exp(-v))


def _rms(v, g):
    return v * lax.rsqrt(jnp.mean(v * v, axis=-1, keepdims=True) + EPS) * g


def _dot(a, b):
    return jnp.dot(a, b, preferred_element_type=F32)


def _conv_steps(buf, w_ref, width, halo, rows, emit):
    steps = []
    for j in range(buf.shape[0]):
        for base in range(0, rows, CONV_ROWS):
            def step(j=j, base=base):
                lanes = slice(j * LANES, (j + 1) * LANES)
                acc = None
                for k in range(width):
                    start = base + halo - (width - 1) + k
                    term = w_ref[k:k + 1, lanes] * buf[j, start:start + CONV_ROWS, :]
                    acc = term if acc is None else acc + term
                emit(slice(base, base + CONV_ROWS), lanes, acc)
            steps.append((width * CONV_ROWS // 8, step))
    return steps


def _packed_conv_steps(packed, w_ref, width, halo, rows, emit):
    group = 2 * SUBLANES
    steps = []
    for j in range(packed.shape[1]):
        for base in range(0, rows, CONV_ROWS):
            def step(j=j, base=base):
                accs = [None] * (CONV_ROWS // group)
                for k in range(width):
                    shift = k % 2
                    tap = w_ref[k, j]
                    for g in range(len(accs)):
                        word = (base + g * group + halo - (width - 1) + k - shift) // 2
                        vals = pltpu.bitcast(packed[shift, j, word:word + SUBLANES, :], BF16)
                        accs[g] = vals * tap if accs[g] is None else accs[g] + vals * tap
                for g, acc in enumerate(accs):
                    lo = base + g * group
                    emit(slice(lo, lo + group), slice(j * LANES, (j + 1) * LANES), acc.astype(F32))
            steps.append((width * CONV_ROWS // 16, step))
    return steps


def _interleave(mxu_steps, vpu_steps):
    total_m = sum(c for c, _ in mxu_steps)
    total_v = sum(c for c, _ in vpu_steps)
    done_m = 0
    done_v = 0
    pending = list(vpu_steps)
    for cost, step in mxu_steps:
        step()
        done_m += cost
        while pending and done_v * total_m < done_m * total_v:
            c, vstep = pending.pop(0)
            vstep()
            done_v += c
    for _, vstep in pending:
        vstep()


def _mixer_kernel(x_ref, g1_ref, win_ref, caw_ref, woa_ref, cbw_ref, cbb_ref, lng_ref, lnb_ref,
                  wob_ref, bg_ref, wo_ref, g2_ref, wr_ref, br_ref, before_ref,
                  x1_ref, lrow_ref, gw_ref, np_ref, hs_ref,
                  abuf, bbuf, bpk, cbuf, pbuf, zbuf, swbuf, *, width_a, width_b, d_model, wa, wb):
    ts = x_ref.shape[0]

    @pl.when(pl.program_id(1) == 0)
    def _():
        abuf[:, 0:HALO_A, :] = jnp.zeros((abuf.shape[0], HALO_A, LANES), F32)
        bbuf[:, 0:HALO_B, :] = jnp.zeros((bbuf.shape[0], HALO_B, LANES), F32)
        bbuf[:, HALO_B + ts:, :] = jnp.zeros((bbuf.shape[0], SUBLANES, LANES), F32)

    h = _rms(x_ref[...], g1_ref[...]).astype(BF16)
    col_b, col_c, col_x = 0, wa, 2 * wa
    col_v, col_g, col_gate = 3 * wa, 3 * wa + wb, 3 * wa + 2 * wb

    def proj(col, q):
        return _dot(h, win_ref[:, col + q * MXU_COLS:col + (q + 1) * MXU_COLS])

    raw_cols = ((col_c, wa), (col_x, wa), (col_b, wa), (col_gate, 2 * d_model))
    pb_c, pb_x, pb_b, pb_g = 0, wa, 2 * wa, 3 * wa
    raw_steps = []
    off = 0
    for col, width in raw_cols:
        for q in range(width // MXU_COLS):
            def raw_step(col=col, q=q, off=off):
                lo = off + q * MXU_COLS
                pbuf[:, lo:lo + MXU_COLS] = proj(col, q)
            raw_steps.append((1, raw_step))
        off += width

    def emit_b(rows, lanes, acc):
        cbuf[rows, lanes] = acc

    conv_steps = _packed_conv_steps(bpk, cbw_ref, width_b, HALO_B, ts, emit_b)

    n_glu = wb // MXU_COLS
    lanes_per_glu = MXU_COLS // LANES
    assert len(RAW_SPLIT) == n_glu + 1 and sum(RAW_SPLIT) == len(raw_steps)
    raw_at = [sum(RAW_SPLIT[:k]) for k in range(len(RAW_SPLIT) + 1)]
    for _, raw in raw_steps[:raw_at[1]]:
        raw()
    for q in range(n_glu):
        u = proj(col_v, q) * _sigmoid(proj(col_g, q))
        for jj in range(lanes_per_glu):
            j = q * lanes_per_glu + jj
            bbuf[j, HALO_B:HALO_B + ts, :] = u[:, jj * LANES:(jj + 1) * LANES]
            for shift in range(2):
                rows_f32 = bbuf[j, shift:shift + HALO_B + ts, :]
                bpk[shift, j] = pltpu.bitcast(rows_f32.astype(BF16), jnp.uint32)
            bbuf[j, 0:HALO_B, :] = bbuf[j, ts:ts + HALO_B, :]
        _interleave(raw_steps[raw_at[q + 1]:raw_at[q + 2]],
                    conv_steps[q * len(conv_steps) // n_glu:(q + 1) * len(conv_steps) // n_glu])

    for j in range(wa // LANES):
        abuf[j, HALO_A:HALO_A + ts, :] = (pbuf[:, pb_c + j * LANES:pb_c + (j + 1) * LANES]
                                           * pbuf[:, pb_x + j * LANES:pb_x + (j + 1) * LANES])

    def emit_a(rows, lanes, acc):
        zbuf[rows, lanes] = (pbuf[rows, pb_b + lanes.start:pb_b + lanes.stop] * acc).astype(BF16)

    for _, step in _conv_steps(abuf, caw_ref, width_a, HALO_A, ts, emit_a):
        step()
    abuf[:, 0:HALO_A, :] = abuf[:, ts:ts + HALO_A, :]

    for r in range(0, ts, TAIL_ROWS):
        rows = slice(r, r + TAIL_ROWS)
        c = cbuf[rows, :] + cbb_ref[...]
        mu = jnp.mean(c, axis=-1, keepdims=True)
        cc = c - mu
        var = jnp.mean(cc * cc, axis=-1, keepdims=True)
        ln = cc * lax.rsqrt(var + EPS) * lng_ref[...] + lnb_ref[...]
        swbuf[rows, :] = (ln * _sigmoid(ln)).astype(BF16)
        gcols = slice(pb_g, pb_g + 2 * d_model)
        pbuf[rows, gcols] = _sigmoid(pbuf[rows, gcols] + bg_ref[...])
    mix = (pbuf[:, pb_g:pb_g + d_model] * _dot(zbuf[...], woa_ref[...])
           + pbuf[:, pb_g + d_model:pb_g + 2 * d_model] * _dot(swbuf[...], wob_ref[...]))

    x1 = x_ref[...] + _dot(mix.astype(BF16), wo_ref[...])
    x1_ref[...] = x1

    h2 = _rms(x1, g2_ref[...])
    logits = lax.dot_general(wr_ref[...], h2, (((1,), (1,)), ((), ())),
                             preferred_element_type=F32) + br_ref[:, 0:1]
    gl = logits[0:N_GROUPS, :]
    gmax = jnp.max(gl, axis=0, keepdims=True)
    giota = lax.broadcasted_iota(jnp.int32, gl.shape, 0)
    grp = jnp.min(jnp.where(gl == gmax, giota, N_GROUPS), axis=0, keepdims=True)
    p_grp = 1.0 / jnp.sum(jnp.exp(gl - gmax), axis=0, keepdims=True)

    el = logits[8:8 + N_EXPERTS, :]
    eiota = lax.broadcasted_iota(jnp.int32, el.shape, 0)
    neg = jnp.float32(-jnp.inf)
    m0 = jnp.where(lax.div(eiota, EXPERTS_PER_GROUP) == grp, el, neg)
    v1 = jnp.max(m0, axis=0, keepdims=True)
    i1 = jnp.min(jnp.where(m0 == v1, eiota, N_EXPERTS), axis=0, keepdims=True)
    m1 = jnp.where(eiota == i1, neg, m0)
    v2 = jnp.max(m1, axis=0, keepdims=True)
    i2 = jnp.min(jnp.where(m1 == v2, eiota, N_EXPERTS), axis=0, keepdims=True)
    e2 = jnp.exp(v2 - v1)
    w1 = p_grp / (1.0 + e2)
    w2 = p_grp * e2 / (1.0 + e2)

    oh1 = eiota == i1
    oh2 = eiota == i2
    used = jnp.where(oh1, 1.0, jnp.where(oh2, 1.0, 0.0))
    in_strip = _dot(used.astype(BF16), before_ref[...])
    pieces = jnp.floor((jnp.sum(used, axis=1, keepdims=True) + (PIECE - 1.0)) * (1.0 / PIECE))
    er = lax.broadcasted_iota(jnp.int32, (N_EXPERTS, N_EXPERTS), 0)
    ec = lax.broadcasted_iota(jnp.int32, (N_EXPERTS, N_EXPERTS), 1)
    earlier = jnp.where(ec < er, 1.0, 0.0)
    strip_start = _dot(earlier, jnp.broadcast_to(pieces * PIECE, (N_EXPERTS, LANES)))[:, 0:1]
    row = strip_start + in_strip
    r1 = jnp.sum(jnp.where(oh1, row, 0.0), axis=0, keepdims=True)
    r2 = jnp.sum(jnp.where(oh2, row, 0.0), axis=0, keepdims=True)

    lrow = jnp.concatenate([r1, r2], axis=0).astype(jnp.int32)
    lrow_ref[...] = lrow
    gw_ref[...] = jnp.concatenate([w1, w2], axis=0)
    np_ref[...] = jnp.broadcast_to(pieces, np_ref.shape).astype(jnp.int32)

    riota = lax.broadcasted_iota(jnp.int32, (hs_ref.shape[0], ts), 0)
    pick = jnp.where(riota == lrow[0:1, :], 1.0, jnp.where(riota == lrow[1:2, :], 1.0, 0.0))
    hs_ref[...] = _dot(pick.astype(BF16), h2.astype(BF16))


def _const_spec(shape):
    return pl.BlockSpec(shape, lambda *_: (0,) * len(shape), pipeline_mode=pl.Buffered(1))


def _mixer(x, g1, w_in, conv_a_w, w_out_a, conv_b_w, conv_b_b, ln_g, ln_b, w_out_b, b_gate, w_o,
           g2, w_r, b_r):
    bsz, seq, d = x.shape
    wa = w_out_a.shape[0]
    wb = w_out_b.shape[0]
    ts = SEQ_TILE
    n_s = seq // ts
    t = bsz * seq
    taps_b = conv_b_w.astype(BF16).reshape(conv_b_w.shape[0], wb // LANES, 1, LANES)
    taps_b = jnp.broadcast_to(taps_b, taps_b.shape[:2] + (2 * SUBLANES, LANES))
    before = jnp.triu(jnp.ones((ts, ts), BF16), k=1)
    consts = (g1, w_in, conv_a_w, w_out_a, taps_b, conv_b_b, ln_g, ln_b, w_out_b, b_gate, w_o,
              g2, w_r, b_r, before)
    tok_spec = pl.BlockSpec((TOP_K, ts), lambda b, s: (0, b * n_s + s))
    return pl.pallas_call(
        functools.partial(_mixer_kernel, width_a=conv_a_w.shape[0], width_b=conv_b_w.shape[0],
                          d_model=d, wa=wa, wb=wb),
        grid=(bsz, n_s),
        in_specs=[pl.BlockSpec((None, ts, d), lambda b, s: (b, s, 0))]
                 + [_const_spec(c.shape) for c in consts],
        out_specs=[pl.BlockSpec((None, ts, d), lambda b, s: (b, s, 0)),
                   tok_spec, tok_spec,
                   pl.BlockSpec((None, N_EXPERTS, LANES), lambda b, s: (b * n_s + s, 0, 0)),
                   pl.BlockSpec((SORTED_ROWS, d), lambda b, s: (b * n_s + s, 0))],
        out_shape=[jax.ShapeDtypeStruct((bsz, seq, d), F32),
                   jax.ShapeDtypeStruct((TOP_K, t), jnp.int32),
                   jax.ShapeDtypeStruct((TOP_K, t), F32),
                   jax.ShapeDtypeStruct((t // ts, N_EXPERTS, LANES), jnp.int32),
                   jax.ShapeDtypeStruct((t // ts * SORTED_ROWS, d), F32)],
        scratch_shapes=[pltpu.VMEM((wa // LANES, HALO_A + ts, LANES), F32),
                        pltpu.VMEM((wb // LANES, HALO_B + ts + SUBLANES, LANES), F32),
                        pltpu.VMEM((2, wb // LANES, (HALO_B + ts) // 2, LANES), jnp.uint32),
                        pltpu.VMEM((ts, wb), F32),
                        pltpu.VMEM((ts, 3 * wa + 2 * d), F32),
                        pltpu.VMEM((ts, wa), BF16),
                        pltpu.VMEM((ts, wb), BF16)],
        compiler_params=pltpu.CompilerParams(
            dimension_semantics=("arbitrary", "arbitrary"), vmem_limit_bytes=MIXER_VMEM_LIMIT),
        name="mixer",
    )(x, *consts)


def _zero_unwritten(used_ref, ys_ref, zero, sem):
    n_tiles = used_ref.shape[0]
    rb = zero.shape[0]

    def fill(row, n_rows):
        return pltpu.make_async_copy(zero.at[pl.ds(0, n_rows), :],
                                     ys_ref.at[pl.ds(row, n_rows), :], sem)

    dump = [fill(r, rb) for r in range(n_tiles * SORTED_ROWS, ys_ref.shape[0], rb)]

    def tails(wait):
        def body(i, carry):
            used = used_ref[i]
            tail = SORTED_ROWS - used
            row = i * SORTED_ROWS + used
            size = SORTED_ROWS - TOP_K * SEQ_TILE
            while size >= PIECE:
                @pl.when((tail & size) != 0)
                def _(row=row, size=size):
                    c = fill(pl.multiple_of(row, PIECE), size)
                    c.wait() if wait else c.start()
                row = row + (tail & size)
                size //= 2
            return carry
        lax.fori_loop(0, n_tiles, body, 0)

    zero[...] = jnp.zeros(zero.shape, F32)
    for c in dump:
        c.start()
    tails(False)
    for c in dump:
        c.wait()
    tails(True)


def _expert_kernel(bexp_ref, nused_ref, src_ref, dst_ref, used_ref, hs_ref, wg_ref, wu_ref, wd_ref,
                   ys_ref, xbuf, ybuf, wg_bf, wu_bf, wd_bf, sem_in, sem_out):
    b = pl.program_id(0)
    n_used = nused_ref[0]
    rb = xbuf.shape[1]

    @pl.when(b == 0)
    def _():
        _zero_unwritten(used_ref, ys_ref, ybuf.at[1], sem_out.at[1])

    def piece_copies(table_ref, blk, hbm_ref, buf, slot, sem, inbound):
        copies = []
        for j in range(PIECES_PER_BLOCK):
            row = pl.multiple_of(table_ref[blk * PIECES_PER_BLOCK + j] * PIECE, PIECE)
            hbm = hbm_ref.at[pl.ds(row, PIECE), :]
            vmem = buf.at[slot, pl.ds(j * PIECE, PIECE), :]
            copies.append(pltpu.make_async_copy(hbm, vmem, sem.at[slot]) if inbound
                          else pltpu.make_async_copy(vmem, hbm, sem.at[slot]))
        return copies

    def start_in(blk, slot):
        for c in piece_copies(src_ref, blk, hs_ref, xbuf, slot, sem_in, True):
            c.start()

    def start_out(blk, slot):
        for c in piece_copies(dst_ref, blk, ys_ref, ybuf, slot, sem_out, False):
            c.start()

    def wait_in(slot):
        pltpu.make_async_copy(hs_ref.at[pl.ds(0, rb), :], xbuf.at[slot], sem_in.at[slot]).wait()

    def wait_out(slot):
        pltpu.make_async_copy(ybuf.at[slot], ys_ref.at[pl.ds(0, rb), :], sem_out.at[slot]).wait()

    @pl.when(b < n_used)
    def _():
        slot = lax.rem(b, 2)
        other = 1 - slot
        prev = jnp.maximum(b - 1, 0)

        @pl.when(b == 0)
        def _():
            start_in(0, 0)

        @pl.when(b + 1 < n_used)
        def _():
            start_in(b + 1, other)

        @pl.when((b == 0) | (bexp_ref[prev] != bexp_ref[b]))
        def _():
            wg_bf[...] = wg_ref[...].astype(BF16)
            wu_bf[...] = wu_ref[...].astype(BF16)
            wd_bf[...] = wd_ref[...].astype(BF16)

        wait_in(slot)

        @pl.when(b >= 2)
        def _():
            wait_out(slot)

        xb = xbuf[slot].astype(BF16)
        a = _dot(xb, wg_bf[...])
        hid = (a * _sigmoid(a) * _dot(xb, wu_bf[...])).astype(BF16)
        ybuf[slot] = _dot(hid, wd_bf[...])
        start_out(b, slot)

        @pl.when(b == n_used - 1)
        def _():
            @pl.when(b >= 1)
            def _():
                wait_out(other)
            wait_out(slot)


def _experts(hs, w_gate, w_up, w_down, bexp, nused, src, dst, used_rows, n_out_rows):
    _, d = hs.shape
    _, _, de = w_gate.shape
    rb = EXPERT_TILE
    assert (n_out_rows - hs.shape[0]) % rb == 0
    wmap = lambda b, bexp, nused, src, dst, used: (bexp[b], 0, 0)
    grid_spec = pltpu.PrefetchScalarGridSpec(
        num_scalar_prefetch=5,
        grid=(bexp.shape[0],),
        in_specs=[pl.BlockSpec(memory_space=pl.ANY),
                  pl.BlockSpec((None, d, de), wmap),
                  pl.BlockSpec((None, d, de), wmap),
                  pl.BlockSpec((None, de, d), wmap)],
        out_specs=pl.BlockSpec(memory_space=pl.ANY),
        scratch_shapes=[pltpu.VMEM((2, rb, d), F32), pltpu.VMEM((2, rb, d), F32),
                        pltpu.VMEM((d, de), BF16), pltpu.VMEM((d, de), BF16),
                        pltpu.VMEM((de, d), BF16),
                        pltpu.SemaphoreType.DMA((2,)), pltpu.SemaphoreType.DMA((2,))],
    )
    return pl.pallas_call(
        _expert_kernel,
        grid_spec=grid_spec,
        out_shape=jax.ShapeDtypeStruct((n_out_rows, d), F32),
        compiler_params=pltpu.CompilerParams(dimension_semantics=("arbitrary",),
                                             vmem_limit_bytes=VMEM_LIMIT),
        name="experts",
    )(bexp, nused, src, dst, used_rows, hs, w_gate, w_up, w_down)


def _combine_kernel(x1_ref, p_ref, lrow_ref, gw_ref, g3_ref, wpg_ref, wpp_ref, gf_ref, ys_ref,
                    out_ref):
    ts = x1_ref.shape[0]
    riota = lax.broadcasted_iota(jnp.int32, (ys_ref.shape[0], ts), 0)
    pick = jnp.where(riota == lrow_ref[0:1, :], gw_ref[0:1, :],
                     jnp.where(riota == lrow_ref[1:2, :], gw_ref[1:2, :], 0.0))
    moe = lax.dot_general(pick.astype(BF16), ys_ref[...].astype(BF16), (((0,), (0,)), ((), ())),
                          preferred_element_type=F32)
    x2 = x1_ref[...] + moe
    hp = _rms(x2, g3_ref[...]).astype(BF16)
    proj = _dot(p_ref[...].astype(BF16), wpp_ref[...])
    x3 = x2 + _sigmoid(_dot(hp, wpg_ref[...])) * proj
    out_ref[...] = _rms(x3, gf_ref[...])


def _combine(x1, p, lrow, gw, ys, g3, w_pg, w_pp, gf):
    t, d = x1.shape
    pd = p.shape[1]
    ts = SEQ_TILE
    tile = lambda i: (i, 0)
    const = lambda i: (0, 0)
    return pl.pallas_call(
        _combine_kernel,
        grid=(t // ts,),
        in_specs=[pl.BlockSpec((ts, d), tile),
                  pl.BlockSpec((ts, pd), tile),
                  pl.BlockSpec((TOP_K, ts), lambda i: (0, i)),
                  pl.BlockSpec((TOP_K, ts), lambda i: (0, i)),
                  pl.BlockSpec((1, d), const),
                  pl.BlockSpec((d, d), const),
                  pl.BlockSpec((pd, d), const),
                  pl.BlockSpec((1, d), const),
                  pl.BlockSpec((SORTED_ROWS, d), tile)],
        out_specs=pl.BlockSpec((ts, d), tile),
        out_shape=jax.ShapeDtypeStruct((t, d), F32),
        compiler_params=pltpu.CompilerParams(dimension_semantics=("arbitrary",),
                                             vmem_limit_bytes=VMEM_LIMIT),
        name="combine",
    )(x1, p, lrow, gw, g3, w_pg, w_pp, gf, ys)


def _piece_plan(n_pieces):
    n_tiles = n_pieces.shape[0]
    ppb = PIECES_PER_BLOCK
    tile_pieces = SORTED_ROWS // PIECE
    i32 = jnp.int32
    excl = lambda v, axis: jnp.cumsum(v, axis=axis, dtype=i32) - v
    base = (jnp.arange(n_tiles, dtype=i32) * tile_pieces)[:, None] + excl(n_pieces, 1)
    per_expert = n_pieces.T
    blocks = (jnp.sum(per_expert, axis=1, dtype=i32) + ppb - 1) // ppb
    first_block = excl(blocks, 0)
    n_used = jnp.sum(blocks, dtype=i32)

    max_blocks = (TOP_K * n_tiles * SEQ_TILE // PIECE + n_tiles * N_EXPERTS) // ppb + N_EXPERTS
    block_ids = jnp.arange(max_blocks, dtype=i32)
    blk = jnp.minimum(block_ids, n_used - 1)
    bexp = jnp.sum((first_block[None, :] <= blk[:, None]).astype(i32), axis=1) - 1
    strip_lo = jnp.take(excl(per_expert, 1), bexp, axis=0)[:, None, :]
    strip_len = jnp.take(per_expert, bexp, axis=0)[:, None, :]
    strip_base = jnp.take(base.T, bexp, axis=0)[:, None, :]
    local = ((block_ids - jnp.take(first_block, bexp)) * ppb)[:, None] + jnp.arange(ppb, dtype=i32)
    local = local[:, :, None]
    inside = (local >= strip_lo) & (local < strip_lo + strip_len)
    src = jnp.sum(jnp.where(inside, strip_base + local - strip_lo, 0), axis=2, dtype=i32).reshape(-1)
    real = jnp.any(inside, axis=2).reshape(-1)
    n_dump = 2 * ppb
    dump = (block_ids % 2)[:, None] * ppb + jnp.arange(ppb, dtype=i32)[None, :]
    dst = jnp.where(real, src, n_tiles * tile_pieces + dump.reshape(-1))
    n_out_rows = (n_tiles * tile_pieces + n_dump) * PIECE
    return bexp, n_used.reshape(1), src, dst, n_out_rows


def _layer(x, p, mix_norm_g, w_in, conv_a_w, w_out_a, conv_b_w, conv_b_b, ln_b_g, ln_b_b, w_out_b,
           b_gate, w_o, ffn_norm_g, w_rg, b_rg, w_re, b_re, w_eg, w_eu, w_ed, ple_norm_g, w_pg, w_pp,
           out_norm_g):
    bsz, seq, d = x.shape
    t = bsz * seq
    row = lambda v: v.reshape(1, -1).astype(F32)
    w_r = jnp.zeros((ROUTER_ROWS, d), F32).at[0:N_GROUPS].set(w_rg.T).at[8:].set(w_re.T)
    b_r = jnp.zeros((ROUTER_ROWS,), F32).at[0:N_GROUPS].set(b_rg).at[8:].set(b_re)
    b_r = jnp.broadcast_to(b_r[:, None], (ROUTER_ROWS, LANES))

    x1, lrow, gw, n_pieces, hs = _mixer(
        x, row(mix_norm_g), w_in.astype(BF16), conv_a_w, w_out_a.astype(BF16), conv_b_w,
        row(conv_b_b), row(ln_b_g), row(ln_b_b), w_out_b.astype(BF16), row(b_gate),
        w_o.astype(BF16), row(ffn_norm_g), w_r, b_r)
    x1 = x1.reshape(t, d)

    n_pieces = n_pieces[:, :, 0]
    bexp, n_used, src, dst, n_out_rows = _piece_plan(n_pieces)
    used_rows = jnp.sum(n_pieces, axis=1, dtype=jnp.int32) * PIECE
    ys = _experts(hs, w_eg, w_eu, w_ed, bexp, n_used, src, dst, used_rows, n_out_rows)
    out = _combine(x1, p.reshape(t, -1), lrow, gw, ys, row(ple_norm_g),
                   w_pg.astype(BF16), w_pp.astype(BF16), row(out_norm_g))
    return out.reshape(bsz, seq, d)


def kernel(x, p, mix_norm_g, w_in, conv_a_w, w_out_a, conv_b_w, conv_b_b, ln_b_g, ln_b_b, w_out_b, b_gate, w_o, ffn_norm_g, w_router_group, b_router_group, w_router_expert, b_router_expert, w_exp_gate, w_exp_up, w_exp_down, ple_norm_g, w_ple_gate, w_ple_proj, final_norm_g):
    assert w_in.shape[0] == 1, "the combine kernel fuses the final RMSNorm: single layer only"
    return _layer(x, p[0], mix_norm_g[0], w_in[0], conv_a_w[0], w_out_a[0], conv_b_w[0],
                  conv_b_b[0], ln_b_g[0], ln_b_b[0], w_out_b[0], b_gate[0], w_o[0], ffn_norm_g[0],
                  w_router_group[0], b_router_group[0], w_router_expert[0], b_router_expert[0],
                  w_exp_gate[0], w_exp_up[0], w_exp_down[0], ple_norm_g[0], w_ple_gate[0],
                  w_ple_proj[0], final_norm_g)
```

```python
import functools

import jax
import jax.numpy as jnp
from jax import lax
from jax.experimental import pallas as pl
from jax.experimental.pallas import tpu as pltpu

F32 = jnp.float32
BF16 = jnp.bfloat16
EPS = 1e-6

LANES = 128
SUBLANES = 8
N_GROUPS = 4
EXPERTS_PER_GROUP = 8
N_EXPERTS = N_GROUPS * EXPERTS_PER_GROUP
TOP_K = 2
ROUTER_ROWS = 8 + N_EXPERTS

SEQ_TILE = 512
CONV_ROWS = 128
MXU_COLS = 256
RAW_SPLIT = (0, 1, 2, 5, 12)
HALO_A = 8
HALO_B = 32
TAIL_ROWS = 128
PIECE = SUBLANES
EXPERT_TILE = 512
PIECES_PER_BLOCK = EXPERT_TILE // PIECE
SORTED_ROWS = TOP_K * SEQ_TILE + N_EXPERTS * PIECE
VMEM_LIMIT = 56 * 1024 * 1024
MIXER_VMEM_LIMIT = 60 * 1024 * 1024


def _sigmoid(v):
    return 1.0 / (1.0 + jnp.exp(-v))


def _rms(v, g):
    return v * lax.rsqrt(jnp.mean(v * v, axis=-1, keepdims=True) + EPS) * g


def _dot(a, b):
    return jnp.dot(a, b, preferred_element_type=F32)


def _conv_steps(buf, w_ref, width, halo, rows, emit):
    steps = []
    for j in range(buf.shape[0]):
        for base in range(0, rows, CONV_ROWS):
            def step(j=j, base=base):
                lanes = slice(j * LANES, (j + 1) * LANES)
                acc = None
                for k in range(width):
                    start = base + halo - (width - 1) + k
                    term = w_ref[k:k + 1, lanes] * buf[j, start:start + CONV_ROWS, :]
                    acc = term if acc is None else acc + term
                emit(slice(base, base + CONV_ROWS), lanes, acc)
            steps.append((width * CONV_ROWS // 8, step))
    return steps


def _packed_conv_steps(packed, w_ref, width, halo, rows, emit):
    group = 2 * SUBLANES
    steps = []
    for j in range(packed.shape[1]):
        for base in range(0, rows, CONV_ROWS):
            def step(j=j, base=base):
                accs = [None] * (CONV_ROWS // group)
                for k in range(width):
                    shift = k % 2
                    tap = w_ref[k, j]
                    for g in range(len(accs)):
                        word = (base + g * group + halo - (width - 1) + k - shift) // 2
                        vals = pltpu.bitcast(packed[shift, j, word:word + SUBLANES, :], BF16)
                        accs[g] = vals * tap if accs[g] is None else accs[g] + vals * tap
                for g, acc in enumerate(accs):
                    lo = base + g * group
                    emit(slice(lo, lo + group), slice(j * LANES, (j + 1) * LANES), acc.astype(F32))
            steps.append((width * CONV_ROWS // 16, step))
    return steps


def _interleave(mxu_steps, vpu_steps):
    total_m = sum(c for c, _ in mxu_steps)
    total_v = sum(c for c, _ in vpu_steps)
    done_m = 0
    done_v = 0
    pending = list(vpu_steps)
    for cost, step in mxu_steps:
        step()
        done_m += cost
        while pending and done_v * total_m < done_m * total_v:
            c, vstep = pending.pop(0)
            vstep()
            done_v += c
    for _, vstep in pending:
        vstep()


def _mixer_kernel(x_ref, g1_ref, win_ref, caw_ref, woa_ref, cbw_ref, cbb_ref, lng_ref, lnb_ref,
                  wob_ref, bg_ref, wo_ref, g2_ref, wr_ref, br_ref, before_ref,
                  x1_ref, lrow_ref, gw_ref, np_ref, hs_ref,
                  abuf, bbuf, bpk, cbuf, pbuf, zbuf, swbuf, *, width_a, width_b, d_model, wa, wb):
    ts = x_ref.shape[0]

    @pl.when(pl.program_id(1) == 0)
    def _():
        abuf[:, 0:HALO_A, :] = jnp.zeros((abuf.shape[0], HALO_A, LANES), F32)
        bbuf[:, 0:HALO_B, :] = jnp.zeros((bbuf.shape[0], HALO_B, LANES), F32)
        bbuf[:, HALO_B + ts:, :] = jnp.zeros((bbuf.shape[0], SUBLANES, LANES), F32)

    h = _rms(x_ref[...], g1_ref[...]).astype(BF16)
    col_b, col_c, col_x = 0, wa, 2 * wa
    col_v, col_g, col_gate = 3 * wa, 3 * wa + wb, 3 * wa + 2 * wb

    def proj(col, q):
        return _dot(h, win_ref[:, col + q * MXU_COLS:col + (q + 1) * MXU_COLS])

    raw_cols = ((col_c, wa), (col_x, wa), (col_b, wa), (col_gate, 2 * d_model))
    pb_c, pb_x, pb_b, pb_g = 0, wa, 2 * wa, 3 * wa
    raw_steps = []
    off = 0
    for col, width in raw_cols:
        for q in range(width // MXU_COLS):
            def raw_step(col=col, q=q, off=off):
                lo = off + q * MXU_COLS
                pbuf[:, lo:lo + MXU_COLS] = proj(col, q)
            raw_steps.append((1, raw_step))
        off += width

    def emit_b(rows, lanes, acc):
        cbuf[rows, lanes] = acc

    conv_steps = _packed_conv_steps(bpk, cbw_ref, width_b, HALO_B, ts, emit_b)

    n_glu = wb // MXU_COLS
    lanes_per_glu = MXU_COLS // LANES
    assert len(RAW_SPLIT) == n_glu + 1 and sum(RAW_SPLIT) == len(raw_steps)
    raw_at = [sum(RAW_SPLIT[:k]) for k in range(len(RAW_SPLIT) + 1)]
    for _, raw in raw_steps[:raw_at[1]]:
        raw()
    for q in range(n_glu):
        u = proj(col_v, q) * _sigmoid(proj(col_g, q))
        for jj in range(lanes_per_glu):
            j = q * lanes_per_glu + jj
            bbuf[j, HALO_B:HALO_B + ts, :] = u[:, jj * LANES:(jj + 1) * LANES]
            for shift in range(2):
                rows_f32 = bbuf[j, shift:shift + HALO_B + ts, :]
                bpk[shift, j] = pltpu.bitcast(rows_f32.astype(BF16), jnp.uint32)
            bbuf[j, 0:HALO_B, :] = bbuf[j, ts:ts + HALO_B, :]
        _interleave(raw_steps[raw_at[q + 1]:raw_at[q + 2]],
                    conv_steps[q * len(conv_steps) // n_glu:(q + 1) * len(conv_steps) // n_glu])

    for j in range(wa // LANES):
        abuf[j, HALO_A:HALO_A + ts, :] = (pbuf[:, pb_c + j * LANES:pb_c + (j + 1) * LANES]
                                           * pbuf[:, pb_x + j * LANES:pb_x + (j + 1) * LANES])

    def emit_a(rows, lanes, acc):
        zbuf[rows, lanes] = (pbuf[rows, pb_b + lanes.start:pb_b + lanes.stop] * acc).astype(BF16)

    for _, step in _conv_steps(abuf, caw_ref, width_a, HALO_A, ts, emit_a):
        step()
    abuf[:, 0:HALO_A, :] = abuf[:, ts:ts + HALO_A, :]

    for r in range(0, ts, TAIL_ROWS):
        rows = slice(r, r + TAIL_ROWS)
        c = cbuf[rows, :] + cbb_ref[...]
        mu = jnp.mean(c, axis=-1, keepdims=True)
        cc = c - mu
        var = jnp.mean(cc * cc, axis=-1, keepdims=True)
        ln = cc * lax.rsqrt(var + EPS) * lng_ref[...] + lnb_ref[...]
        swbuf[rows, :] = (ln * _sigmoid(ln)).astype(BF16)
        gcols = slice(pb_g, pb_g + 2 * d_model)
        pbuf[rows, gcols] = _sigmoid(pbuf[rows, gcols] + bg_ref[...])
    mix = (pbuf[:, pb_g:pb_g + d_model] * _dot(zbuf[...], woa_ref[...])
           + pbuf[:, pb_g + d_model:pb_g + 2 * d_model] * _dot(swbuf[...], wob_ref[...]))

    x1 = x_ref[...] + _dot(mix.astype(BF16), wo_ref[...])
    x1_ref[...] = x1

    h2 = _rms(x1, g2_ref[...])
    logits = lax.dot_general(wr_ref[...], h2, (((1,), (1,)), ((), ())),
                             preferred_element_type=F32) + br_ref[:, 0:1]
    gl = logits[0:N_GROUPS, :]
    gmax = jnp.max(gl, axis=0, keepdims=True)
    giota = lax.broadcasted_iota(jnp.int32, gl.shape, 0)
    grp = jnp.min(jnp.where(gl == gmax, giota, N_GROUPS), axis=0, keepdims=True)
    p_grp = 1.0 / jnp.sum(jnp.exp(gl - gmax), axis=0, keepdims=True)

    el = logits[8:8 + N_EXPERTS, :]
    eiota = lax.broadcasted_iota(jnp.int32, el.shape, 0)
    neg = jnp.float32(-jnp.inf)
    m0 = jnp.where(lax.div(eiota, EXPERTS_PER_GROUP) == grp, el, neg)
    v1 = jnp.max(m0, axis=0, keepdims=True)
    i1 = jnp.min(jnp.where(m0 == v1, eiota, N_EXPERTS), axis=0, keepdims=True)
    m1 = jnp.where(eiota == i1, neg, m0)
    v2 = jnp.max(m1, axis=0, keepdims=True)
    i2 = jnp.min(jnp.where(m1 == v2, eiota, N_EXPERTS), axis=0, keepdims=True)
    e2 = jnp.exp(v2 - v1)
    w1 = p_grp / (1.0 + e2)
    w2 = p_grp * e2 / (1.0 + e2)

    oh1 = eiota == i1
    oh2 = eiota == i2
    used = jnp.where(oh1, 1.0, jnp.where(oh2, 1.0, 0.0))
    in_strip = _dot(used.astype(BF16), before_ref[...])
    pieces = jnp.floor((jnp.sum(used, axis=1, keepdims=True) + (PIECE - 1.0)) * (1.0 / PIECE))
    er = lax.broadcasted_iota(jnp.int32, (N_EXPERTS, N_EXPERTS), 0)
    ec = lax.broadcasted_iota(jnp.int32, (N_EXPERTS, N_EXPERTS), 1)
    earlier = jnp.where(ec < er, 1.0, 0.0)
    strip_start = _dot(earlier, jnp.broadcast_to(pieces * PIECE, (N_EXPERTS, LANES)))[:, 0:1]
    row = strip_start + in_strip
    r1 = jnp.sum(jnp.where(oh1, row, 0.0), axis=0, keepdims=True)
    r2 = jnp.sum(jnp.where(oh2, row, 0.0), axis=0, keepdims=True)

    lrow = jnp.concatenate([r1, r2], axis=0).astype(jnp.int32)
    lrow_ref[...] = lrow
    gw_ref[...] = jnp.concatenate([w1, w2], axis=0)
    np_ref[...] = jnp.broadcast_to(pieces, np_ref.shape).astype(jnp.int32)

    riota = lax.broadcasted_iota(jnp.int32, (hs_ref.shape[0], ts), 0)
    pick = jnp.where(riota == lrow[0:1, :], 1.0, jnp.where(riota == lrow[1:2, :], 1.0, 0.0))
    hs_ref[...] = _dot(pick.astype(BF16), h2.astype(BF16))


def _const_spec(shape):
    return pl.BlockSpec(shape, lambda *_: (0,) * len(shape), pipeline_mode=pl.Buffered(1))


def _mixer(x, g1, w_in, conv_a_w, w_out_a, conv_b_w, conv_b_b, ln_g, ln_b, w_out_b, b_gate, w_o,
           g2, w_r, b_r):
    bsz, seq, d = x.shape
    wa = w_out_a.shape[0]
    wb = w_out_b.shape[0]
    ts = SEQ_TILE
    n_s = seq // ts
    t = bsz * seq
    taps_b = conv_b_w.astype(BF16).reshape(conv_b_w.shape[0], wb // LANES, 1, LANES)
    taps_b = jnp.broadcast_to(taps_b, taps_b.shape[:2] + (2 * SUBLANES, LANES))
    before = jnp.triu(jnp.ones((ts, ts), BF16), k=1)
    consts = (g1, w_in, conv_a_w, w_out_a, taps_b, conv_b_b, ln_g, ln_b, w_out_b, b_gate, w_o,
              g2, w_r, b_r, before)
    tok_spec = pl.BlockSpec((TOP_K, ts), lambda b, s: (0, b * n_s + s))
    return pl.pallas_call(
        functools.partial(_mixer_kernel, width_a=conv_a_w.shape[0], width_b=conv_b_w.shape[0],
                          d_model=d, wa=wa, wb=wb),
        grid=(bsz, n_s),
        in_specs=[pl.BlockSpec((None, ts, d), lambda b, s: (b, s, 0))]
                 + [_const_spec(c.shape) for c in consts],
        out_specs=[pl.BlockSpec((None, ts, d), lambda b, s: (b, s, 0)),
                   tok_spec, tok_spec,
                   pl.BlockSpec((None, N_EXPERTS, LANES), lambda b, s: (b * n_s + s, 0, 0)),
                   pl.BlockSpec((SORTED_ROWS, d), lambda b, s: (b * n_s + s, 0))],
        out_shape=[jax.ShapeDtypeStruct((bsz, seq, d), F32),
                   jax.ShapeDtypeStruct((TOP_K, t), jnp.int32),
                   jax.ShapeDtypeStruct((TOP_K, t), F32),
                   jax.ShapeDtypeStruct((t // ts, N_EXPERTS, LANES), jnp.int32),
                   jax.ShapeDtypeStruct((t // ts * SORTED_ROWS, d), F32)],
        scratch_shapes=[pltpu.VMEM((wa // LANES, HALO_A + ts, LANES), F32),
                        pltpu.VMEM((wb // LANES, HALO_B + ts + SUBLANES, LANES), F32),
                        pltpu.VMEM((2, wb // LANES, (HALO_B + ts) // 2, LANES), jnp.uint32),
                        pltpu.VMEM((ts, wb), F32),
                        pltpu.VMEM((ts, 3 * wa + 2 * d), F32),
                        pltpu.VMEM((ts, wa), BF16),
                        pltpu.VMEM((ts, wb), BF16)],
        compiler_params=pltpu.CompilerParams(
            dimension_semantics=("arbitrary", "arbitrary"), vmem_limit_bytes=MIXER_VMEM_LIMIT),
        name="mixer",
    )(x, *consts)


def _zero_unwritten(used_ref, ys_ref, zero, sem):
    n_tiles = used_ref.shape[0]
    rb = zero.shape[0]

    def fill(row, n_rows):
        return pltpu.make_async_copy(zero.at[pl.ds(0, n_rows), :],
                                     ys_ref.at[pl.ds(row, n_rows), :], sem)

    dump = [fill(r, rb) for r in range(n_tiles * SORTED_ROWS, ys_ref.shape[0], rb)]

    def tails(wait):
        def body(i, carry):
            used = used_ref[i]
            tail = SORTED_ROWS - used
            row = i * SORTED_ROWS + used
            size = SORTED_ROWS - TOP_K * SEQ_TILE
            while size >= PIECE:
                @pl.when((tail & size) != 0)
                def _(row=row, size=size):
                    c = fill(pl.multiple_of(row, PIECE), size)
                    c.wait() if wait else c.start()
                row = row + (tail & size)
                size //= 2
            return carry
        lax.fori_loop(0, n_tiles, body, 0)

    zero[...] = jnp.zeros(zero.shape, F32)
    for c in dump:
        c.start()
    tails(False)
    for c in dump:
        c.wait()
    tails(True)


def _expert_kernel(bexp_ref, nused_ref, src_ref, dst_ref, used_ref, hs_ref, wg_ref, wu_ref, wd_ref,
                   ys_ref, xbuf, ybuf, wg_bf, wu_bf, wd_bf, sem_in, sem_out):
    b = pl.program_id(0)
    n_used = nused_ref[0]
    rb = xbuf.shape[1]

    @pl.when(b == 0)
    def _():
        _zero_unwritten(used_ref, ys_ref, ybuf.at[1], sem_out.at[1])

    def piece_copies(table_ref, blk, hbm_ref, buf, slot, sem, inbound):
        copies = []
        for j in range(PIECES_PER_BLOCK):
            row = pl.multiple_of(table_ref[blk * PIECES_PER_BLOCK + j] * PIECE, PIECE)
            hbm = hbm_ref.at[pl.ds(row, PIECE), :]
            vmem = buf.at[slot, pl.ds(j * PIECE, PIECE), :]
            copies.append(pltpu.make_async_copy(hbm, vmem, sem.at[slot]) if inbound
                          else pltpu.make_async_copy(vmem, hbm, sem.at[slot]))
        return copies

    def start_in(blk, slot):
        for c in piece_copies(src_ref, blk, hs_ref, xbuf, slot, sem_in, True):
            c.start()

    def start_out(blk, slot):
        for c in piece_copies(dst_ref, blk, ys_ref, ybuf, slot, sem_out, False):
            c.start()

    def wait_in(slot):
        pltpu.make_async_copy(hs_ref.at[pl.ds(0, rb), :], xbuf.at[slot], sem_in.at[slot]).wait()

    def wait_out(slot):
        pltpu.make_async_copy(ybuf.at[slot], ys_ref.at[pl.ds(0, rb), :], sem_out.at[slot]).wait()

    @pl.when(b < n_used)
    def _():
        slot = lax.rem(b, 2)
        other = 1 - slot
        prev = jnp.maximum(b - 1, 0)

        @pl.when(b == 0)
        def _():
            start_in(0, 0)

        @pl.when(b + 1 < n_used)
        def _():
            start_in(b + 1, other)

        @pl.when((b == 0) | (bexp_ref[prev] != bexp_ref[b]))
        def _():
            wg_bf[...] = wg_ref[...].astype(BF16)
            wu_bf[...] = wu_ref[...].astype(BF16)
            wd_bf[...] = wd_ref[...].astype(BF16)

        wait_in(slot)

        @pl.when(b >= 2)
        def _():
            wait_out(slot)

        xb = xbuf[slot].astype(BF16)
        a = _dot(xb, wg_bf[...])
        hid = (a * _sigmoid(a) * _dot(xb, wu_bf[...])).astype(BF16)
        ybuf[slot] = _dot(hid, wd_bf[...])
        start_out(b, slot)

        @pl.when(b == n_used - 1)
        def _():
            @pl.when(b >= 1)
            def _():
                wait_out(other)
            wait_out(slot)


def _experts(hs, w_gate, w_up, w_down, bexp, nused, src, dst, used_rows, n_out_rows):
    _, d = hs.shape
    _, _, de = w_gate.shape
    rb = EXPERT_TILE
    assert (n_out_rows - hs.shape[0]) % rb == 0
    wmap = lambda b, bexp, nused, src, dst, used: (bexp[b], 0, 0)
    grid_spec = pltpu.PrefetchScalarGridSpec(
        num_scalar_prefetch=5,
        grid=(bexp.shape[0],),
        in_specs=[pl.BlockSpec(memory_space=pl.ANY),
                  pl.BlockSpec((None, d, de), wmap),
                  pl.BlockSpec((None, d, de), wmap),
                  pl.BlockSpec((None, de, d), wmap)],
        out_specs=pl.BlockSpec(memory_space=pl.ANY),
        scratch_shapes=[pltpu.VMEM((2, rb, d), F32), pltpu.VMEM((2, rb, d), F32),
                        pltpu.VMEM((d, de), BF16), pltpu.VMEM((d, de), BF16),
                        pltpu.VMEM((de, d), BF16),
                        pltpu.SemaphoreType.DMA((2,)), pltpu.SemaphoreType.DMA((2,))],
    )
    return pl.pallas_call(
        _expert_kernel,
        grid_spec=grid_spec,
        out_shape=jax.ShapeDtypeStruct((n_out_rows, d), F32),
        compiler_params=pltpu.CompilerParams(dimension_semantics=("arbitrary",),
                                             vmem_limit_bytes=VMEM_LIMIT),
        name="experts",
    )(bexp, nused, src, dst, used_rows, hs, w_gate, w_up, w_down)


def _combine_kernel(x1_ref, p_ref, lrow_ref, gw_ref, g3_ref, wpg_ref, wpp_ref, gf_ref, ys_ref,
                    out_ref):
    ts = x1_ref.shape[0]
    riota = lax.broadcasted_iota(jnp.int32, (ys_ref.shape[0], ts), 0)
    pick = jnp.where(riota == lrow_ref[0:1, :], gw_ref[0:1, :],
                     jnp.where(riota == lrow_ref[1:2, :], gw_ref[1:2, :], 0.0))
    moe = lax.dot_general(pick.astype(BF16), ys_ref[...].astype(BF16), (((0,), (0,)), ((), ())),
                          preferred_element_type=F32)
    x2 = x1_ref[...] + moe
    hp = _rms(x2, g3_ref[...]).astype(BF16)
    proj = _dot(p_ref[...].astype(BF16), wpp_ref[...])
    x3 = x2 + _sigmoid(_dot(hp, wpg_ref[...])) * proj
    out_ref[...] = _rms(x3, gf_ref[...])


def _combine(x1, p, lrow, gw, ys, g3, w_pg, w_pp, gf):
    t, d = x1.shape
    pd = p.shape[1]
    ts = SEQ_TILE
    tile = lambda i: (i, 0)
    const = lambda i: (0, 0)
    return pl.pallas_call(
        _combine_kernel,
        grid=(t // ts,),
        in_specs=[pl.BlockSpec((ts, d), tile),
                  pl.BlockSpec((ts, pd), tile),
                  pl.BlockSpec((TOP_K, ts), lambda i: (0, i)),
                  pl.BlockSpec((TOP_K, ts), lambda i: (0, i)),
                  pl.BlockSpec((1, d), const),
                  pl.BlockSpec((d, d), const),
                  pl.BlockSpec((pd, d), const),
                  pl.BlockSpec((1, d), const),
                  pl.BlockSpec((SORTED_ROWS, d), tile)],
        out_specs=pl.BlockSpec((ts, d), tile),
        out_shape=jax.ShapeDtypeStruct((t, d), F32),
        compiler_params=pltpu.CompilerParams(dimension_semantics=("arbitrary",),
                                             vmem_limit_bytes=VMEM_LIMIT),
        name="combine",
    )(x1, p, lrow, gw, g3, w_pg, w_pp, gf, ys)


def _piece_plan(n_pieces):
    n_tiles = n_pieces.shape[0]
    ppb = PIECES_PER_BLOCK
    tile_pieces = SORTED_ROWS // PIECE
    i32 = jnp.int32
    excl = lambda v, axis: jnp.cumsum(v, axis=axis, dtype=i32) - v
    base = (jnp.arange(n_tiles, dtype=i32) * tile_pieces)[:, None] + excl(n_pieces, 1)
    per_expert = n_pieces.T
    blocks = (jnp.sum(per_expert, axis=1, dtype=i32) + ppb - 1) // ppb
    first_block = excl(blocks, 0)
    n_used = jnp.sum(blocks, dtype=i32)

    max_blocks = (TOP_K * n_tiles * SEQ_TILE // PIECE + n_tiles * N_EXPERTS) // ppb + N_EXPERTS
    block_ids = jnp.arange(max_blocks, dtype=i32)
    blk = jnp.minimum(block_ids, n_used - 1)
    bexp = jnp.sum((first_block[None, :] <= blk[:, None]).astype(i32), axis=1) - 1
    strip_lo = jnp.take(excl(per_expert, 1), bexp, axis=0)[:, :, None]
    strip_len = jnp.take(per_expert, bexp, axis=0)[:, :, None]
    strip_base = jnp.take(base.T, bexp, axis=0)[:, :, None]
    local = ((block_ids - jnp.take(first_block, bexp)) * ppb)[:, None] + jnp.arange(ppb, dtype=i32)
    local = local[:, None, :]
    inside = (local >= strip_lo) & (local < strip_lo + strip_len)
    src = jnp.sum(jnp.where(inside, strip_base + local - strip_lo, 0), axis=1, dtype=i32).reshape(-1)
    real = jnp.any(inside, axis=1).reshape(-1)
    n_dump = 2 * ppb
    dump = (block_ids % 2)[:, None] * ppb + jnp.arange(ppb, dtype=i32)[None, :]
    dst = jnp.where(real, src, n_tiles * tile_pieces + dump.reshape(-1))
    n_out_rows = (n_tiles * tile_pieces + n_dump) * PIECE
    return bexp, n_used.reshape(1), src, dst, n_out_rows


def _layer(x, p, mix_norm_g, w_in, conv_a_w, w_out_a, conv_b_w, conv_b_b, ln_b_g, ln_b_b, w_out_b,
           b_gate, w_o, ffn_norm_g, w_rg, b_rg, w_re, b_re, w_eg, w_eu, w_ed, ple_norm_g, w_pg, w_pp,
           out_norm_g):
    bsz, seq, d = x.shape
    t = bsz * seq
    row = lambda v: v.reshape(1, -1).astype(F32)
    w_r = jnp.zeros((ROUTER_ROWS, d), F32).at[0:N_GROUPS].set(w_rg.T).at[8:].set(w_re.T)
    b_r = jnp.zeros((ROUTER_ROWS,), F32).at[0:N_GROUPS].set(b_rg).at[8:].set(b_re)
    b_r = jnp.broadcast_to(b_r[:, None], (ROUTER_ROWS, LANES))

    x1, lrow, gw, n_pieces, hs = _mixer(
        x, row(mix_norm_g), w_in.astype(BF16), conv_a_w, w_out_a.astype(BF16), conv_b_w,
        row(conv_b_b), row(ln_b_g), row(ln_b_b), w_out_b.astype(BF16), row(b_gate),
        w_o.astype(BF16), row(ffn_norm_g), w_r, b_r)
    x1 = x1.reshape(t, d)

    n_pieces = n_pieces[:, :, 0]
    bexp, n_used, src, dst, n_out_rows = _piece_plan(n_pieces)
    used_rows = jnp.sum(n_pieces, axis=1, dtype=jnp.int32) * PIECE
    ys = _experts(hs, w_eg, w_eu, w_ed, bexp, n_used, src, dst, used_rows, n_out_rows)
    out = _combine(x1, p.reshape(t, -1), lrow, gw, ys, row(ple_norm_g),
                   w_pg.astype(BF16), w_pp.astype(BF16), row(out_norm_g))
    return out.reshape(bsz, seq, d)


def kernel(x, p, mix_norm_g, w_in, conv_a_w, w_out_a, conv_b_w, conv_b_b, ln_b_g, ln_b_b, w_out_b, b_gate, w_o, ffn_norm_g, w_router_group, b_router_group, w_router_expert, b_router_expert, w_exp_gate, w_exp_up, w_exp_down, ple_norm_g, w_ple_gate, w_ple_proj, final_norm_g):
    assert w_in.shape[0] == 1, "the combine kernel fuses the final RMSNorm: single layer only"
    return _layer(x, p[0], mix_norm_g[0], w_in[0], conv_a_w[0], w_out_a[0], conv_b_w[0],
                  conv_b_b[0], ln_b_g[0], ln_b_b[0], w_out_b[0], b_gate[0], w_o[0], ffn_norm_g[0],
                  w_router_group[0], b_router_group[0], w_router_expert[0], b_router_expert[0],
                  w_exp_gate[0], w_exp_up[0], w_exp_down[0], ple_norm_g[0], w_ple_gate[0],
                  w_ple_proj[0], final_norm_g)
```

```python
import functools

import jax
import jax.numpy as jnp
from jax import lax
from jax.experimental import pallas as pl
from jax.experimental.pallas import tpu as pltpu

F32 = jnp.float32
BF16 = jnp.bfloat16
EPS = 1e-6

LANES = 128
SUBLANES = 8
N_GROUPS = 4
EXPERTS_PER_GROUP = 8
N_EXPERTS = N_GROUPS * EXPERTS_PER_GROUP
TOP_K = 2
ROUTER_ROWS = 8 + N_EXPERTS

SEQ_TILE = 512
CONV_ROWS = 128
MXU_COLS = 256
RAW_SPLIT = (0, 1, 2, 5, 12)
HALO_A = 8
HALO_B = 32
TAIL_ROWS = 128
PIECE = SUBLANES
EXPERT_TILE = 512
PIECES_PER_BLOCK = EXPERT_TILE // PIECE
SORTED_ROWS = TOP_K * SEQ_TILE + N_EXPERTS * PIECE
VMEM_LIMIT = 56 * 1024 * 1024
MIXER_VMEM_LIMIT = 60 * 1024 * 1024


def _sigmoid(v):
    return 1.0 / (1.0 + jnp.exp(-v))


def _rms(v, g):
    return v * lax.rsqrt(jnp.mean(v * v, axis=-1, keepdims=True) + EPS) * g


def _dot(a, b):
    return jnp.dot(a, b, preferred_element_type=F32)


def _conv_steps(buf, w_ref, width, halo, rows, emit):
    steps = []
    for j in range(buf.shape[0]):
        for base in range(0, rows, CONV_ROWS):
            def step(j=j, base=base):
                lanes = slice(j * LANES, (j + 1) * LANES)
                acc = None
                for k in range(width):
                    start = base + halo - (width - 1) + k
                    term = w_ref[k:k + 1, lanes] * buf[j, start:start + CONV_ROWS, :]
                    acc = term if acc is None else acc + term
                emit(slice(base, base + CONV_ROWS), lanes, acc)
            steps.append((width * CONV_ROWS // 8, step))
    return steps


def _packed_conv_steps(packed, w_ref, width, halo, rows, emit):
    group = 2 * SUBLANES
    steps = []
    for j in range(packed.shape[1]):
        for base in range(0, rows, CONV_ROWS):
            def step(j=j, base=base):
                accs = [None] * (CONV_ROWS // group)
                for k in range(width):
                    shift = k % 2
                    tap = w_ref[k, j]
                    for g in range(len(accs)):
                        word = (base + g * group + halo - (width - 1) + k - shift) // 2
                        vals = pltpu.bitcast(packed[shift, j, word:word + SUBLANES, :], BF16)
                        accs[g] = vals * tap if accs[g] is None else accs[g] + vals * tap
                for g, acc in enumerate(accs):
                    lo = base + g * group
                    emit(slice(lo, lo + group), slice(j * LANES, (j + 1) * LANES), acc.astype(F32))
            steps.append((width * CONV_ROWS // 16, step))
    return steps


def _interleave(mxu_steps, vpu_steps):
    total_m = sum(c for c, _ in mxu_steps)
    total_v = sum(c for c, _ in vpu_steps)
    done_m = 0
    done_v = 0
    pending = list(vpu_steps)
    for cost, step in mxu_steps:
        step()
        done_m += cost
        while pending and done_v * total_m < done_m * total_v:
            c, vstep = pending.pop(0)
            vstep()
            done_v += c
    for _, vstep in pending:
        vstep()


def _mixer_kernel(x_ref, g1_ref, win_ref, caw_ref, woa_ref, cbw_ref, cbb_ref, lng_ref, lnb_ref,
                  wob_ref, bg_ref, wo_ref, g2_ref, wr_ref, br_ref, before_ref,
                  x1_ref, lrow_ref, gw_ref, np_ref, hs_ref,
                  abuf, bbuf, bpk, cbuf, pbuf, zbuf, swbuf, *, width_a, width_b, d_model, wa, wb):
    ts = x_ref.shape[0]

    @pl.when(pl.program_id(1) == 0)
    def _():
        abuf[:, 0:HALO_A, :] = jnp.zeros((abuf.shape[0], HALO_A, LANES), F32)
        bbuf[:, 0:HALO_B, :] = jnp.zeros((bbuf.shape[0], HALO_B, LANES), F32)
        bbuf[:, HALO_B + ts:, :] = jnp.zeros((bbuf.shape[0], SUBLANES, LANES), F32)

    h = _rms(x_ref[...], g1_ref[...]).astype(BF16)
    col_b, col_c, col_x = 0, wa, 2 * wa
    col_v, col_g, col_gate = 3 * wa, 3 * wa + wb, 3 * wa + 2 * wb

    def proj(col, q):
        return _dot(h, win_ref[:, col + q * MXU_COLS:col + (q + 1) * MXU_COLS])

    raw_cols = ((col_c, wa), (col_x, wa), (col_b, wa), (col_gate, 2 * d_model))
    pb_c, pb_x, pb_b, pb_g = 0, wa, 2 * wa, 3 * wa
    raw_steps = []
    off = 0
    for col, width in raw_cols:
        for q in range(width // MXU_COLS):
            def raw_step(col=col, q=q, off=off):
                lo = off + q * MXU_COLS
                pbuf[:, lo:lo + MXU_COLS] = proj(col, q)
            raw_steps.append((1, raw_step))
        off += width

    def emit_b(rows, lanes, acc):
        cbuf[rows, lanes] = acc

    conv_steps = _packed_conv_steps(bpk, cbw_ref, width_b, HALO_B, ts, emit_b)

    n_glu = wb // MXU_COLS
    lanes_per_glu = MXU_COLS // LANES
    assert len(RAW_SPLIT) == n_glu + 1 and sum(RAW_SPLIT) == len(raw_steps)
    raw_at = [sum(RAW_SPLIT[:k]) for k in range(len(RAW_SPLIT) + 1)]
    for _, raw in raw_steps[:raw_at[1]]:
        raw()
    for q in range(n_glu):
        u = proj(col_v, q) * _sigmoid(proj(col_g, q))
        for jj in range(lanes_per_glu):
            j = q * lanes_per_glu + jj
            bbuf[j, HALO_B:HALO_B + ts, :] = u[:, jj * LANES:(jj + 1) * LANES]
            for shift in range(2):
                rows_f32 = bbuf[j, shift:shift + HALO_B + ts, :]
                bpk[shift, j] = pltpu.bitcast(rows_f32.astype(BF16), jnp.uint32)
            bbuf[j, 0:HALO_B, :] = bbuf[j, ts:ts + HALO_B, :]
        _interleave(raw_steps[raw_at[q + 1]:raw_at[q + 2]],
                    conv_steps[q * len(conv_steps) // n_glu:(q + 1) * len(conv_steps) // n_glu])

    for j in range(wa // LANES):
        abuf[j, HALO_A:HALO_A + ts, :] = (pbuf[:, pb_c + j * LANES:pb_c + (j + 1) * LANES]
                                           * pbuf[:, pb_x + j * LANES:pb_x + (j + 1) * LANES])

    def emit_a(rows, lanes, acc):
        zbuf[rows, lanes] = (pbuf[rows, pb_b + lanes.start:pb_b + lanes.stop] * acc).astype(BF16)

    for _, step in _conv_steps(abuf, caw_ref, width_a, HALO_A, ts, emit_a):
        step()
    abuf[:, 0:HALO_A, :] = abuf[:, ts:ts + HALO_A, :]

    for r in range(0, ts, TAIL_ROWS):
        rows = slice(r, r + TAIL_ROWS)
        c = cbuf[rows, :] + cbb_ref[...]
        mu = jnp.mean(c, axis=-1, keepdims=True)
        cc = c - mu
        var = jnp.mean(cc * cc, axis=-1, keepdims=True)
        ln = cc * lax.rsqrt(var + EPS) * lng_ref[...] + lnb_ref[...]
        swbuf[rows, :] = (ln * _sigmoid(ln)).astype(BF16)
        gcols = slice(pb_g, pb_g + 2 * d_model)
        pbuf[rows, gcols] = _sigmoid(pbuf[rows, gcols] + bg_ref[...])
    mix = (pbuf[:, pb_g:pb_g + d_model] * _dot(zbuf[...], woa_ref[...])
           + pbuf[:, pb_g + d_model:pb_g + 2 * d_model] * _dot(swbuf[...], wob_ref[...]))

    x1 = x_ref[...] + _dot(mix.astype(BF16), wo_ref[...])
    x1_ref[...] = x1

    h2 = _rms(x1, g2_ref[...])
    logits = lax.dot_general(wr_ref[...], h2, (((1,), (1,)), ((), ())),
                             preferred_element_type=F32) + br_ref[:, 0:1]
    gl = logits[0:N_GROUPS, :]
    gmax = jnp.max(gl, axis=0, keepdims=True)
    giota = lax.broadcasted_iota(jnp.int32, gl.shape, 0)
    grp = jnp.min(jnp.where(gl == gmax, giota, N_GROUPS), axis=0, keepdims=True)
    p_grp = 1.0 / jnp.sum(jnp.exp(gl - gmax), axis=0, keepdims=True)

    el = logits[8:8 + N_EXPERTS, :]
    eiota = lax.broadcasted_iota(jnp.int32, el.shape, 0)
    neg = jnp.float32(-jnp.inf)
    m0 = jnp.where(lax.div(eiota, EXPERTS_PER_GROUP) == grp, el, neg)
    v1 = jnp.max(m0, axis=0, keepdims=True)
    i1 = jnp.min(jnp.where(m0 == v1, eiota, N_EXPERTS), axis=0, keepdims=True)
    m1 = jnp.where(eiota == i1, neg, m0)
    v2 = jnp.max(m1, axis=0, keepdims=True)
    i2 = jnp.min(jnp.where(m1 == v2, eiota, N_EXPERTS), axis=0, keepdims=True)
    e2 = jnp.exp(v2 - v1)
    w1 = p_grp / (1.0 + e2)
    w2 = p_grp * e2 / (1.0 + e2)

    oh1 = eiota == i1
    oh2 = eiota == i2
    used = jnp.where(oh1, 1.0, jnp.where(oh2, 1.0, 0.0))
    in_strip = _dot(used.astype(BF16), before_ref[...])
    pieces = jnp.floor((jnp.sum(used, axis=1, keepdims=True) + (PIECE - 1.0)) * (1.0 / PIECE))
    er = lax.broadcasted_iota(jnp.int32, (N_EXPERTS, N_EXPERTS), 0)
    ec = lax.broadcasted_iota(jnp.int32, (N_EXPERTS, N_EXPERTS), 1)
    earlier = jnp.where(ec < er, 1.0, 0.0)
    strip_start = _dot(earlier, jnp.broadcast_to(pieces * PIECE, (N_EXPERTS, LANES)))[:, 0:1]
    row = strip_start + in_strip
    r1 = jnp.sum(jnp.where(oh1, row, 0.0), axis=0, keepdims=True)
    r2 = jnp.sum(jnp.where(oh2, row, 0.0), axis=0, keepdims=True)

    lrow = jnp.concatenate([r1, r2], axis=0).astype(jnp.int32)
    lrow_ref[...] = lrow
    gw_ref[...] = jnp.concatenate([w1, w2], axis=0)
    np_ref[...] = jnp.broadcast_to(pieces, np_ref.shape).astype(jnp.int32)

    riota = lax.broadcasted_iota(jnp.int32, (hs_ref.shape[0], ts), 0)
    pick = jnp.where(riota == lrow[0:1, :], 1.0, jnp.where(riota == lrow[1:2, :], 1.0, 0.0))
    hs_ref[...] = _dot(pick.astype(BF16), h2.astype(BF16))


def _const_spec(shape):
    return pl.BlockSpec(shape, lambda *_: (0,) * len(shape), pipeline_mode=pl.Buffered(1))


def _mixer(x, g1, w_in, conv_a_w, w_out_a, conv_b_w, conv_b_b, ln_g, ln_b, w_out_b, b_gate, w_o,
           g2, w_r, b_r):
    bsz, seq, d = x.shape
    wa = w_out_a.shape[0]
    wb = w_out_b.shape[0]
    ts = SEQ_TILE
    n_s = seq // ts
    t = bsz * seq
    taps_b = conv_b_w.astype(BF16).reshape(conv_b_w.shape[0], wb // LANES, 1, LANES)
    taps_b = jnp.broadcast_to(taps_b, taps_b.shape[:2] + (2 * SUBLANES, LANES))
    before = jnp.triu(jnp.ones((ts, ts), BF16), k=1)
    consts = (g1, w_in, conv_a_w, w_out_a, taps_b, conv_b_b, ln_g, ln_b, w_out_b, b_gate, w_o,
              g2, w_r, b_r, before)
    tok_spec = pl.BlockSpec((TOP_K, ts), lambda b, s: (0, b * n_s + s))
    return pl.pallas_call(
        functools.partial(_mixer_kernel, width_a=conv_a_w.shape[0], width_b=conv_b_w.shape[0],
                          d_model=d, wa=wa, wb=wb),
        grid=(bsz, n_s),
        in_specs=[pl.BlockSpec((None, ts, d), lambda b, s: (b, s, 0))]
                 + [_const_spec(c.shape) for c in consts],
        out_specs=[pl.BlockSpec((None, ts, d), lambda b, s: (b, s, 0)),
                   tok_spec, tok_spec,
                   pl.BlockSpec((None, N_EXPERTS, LANES), lambda b, s: (b * n_s + s, 0, 0)),
                   pl.BlockSpec((SORTED_ROWS, d), lambda b, s: (b * n_s + s, 0))],
        out_shape=[jax.ShapeDtypeStruct((bsz, seq, d), F32),
                   jax.ShapeDtypeStruct((TOP_K, t), jnp.int32),
                   jax.ShapeDtypeStruct((TOP_K, t), F32),
                   jax.ShapeDtypeStruct((t // ts, N_EXPERTS, LANES), jnp.int32),
                   jax.ShapeDtypeStruct((t // ts * SORTED_ROWS, d), F32)],
        scratch_shapes=[pltpu.VMEM((wa // LANES, HALO_A + ts, LANES), F32),
                        pltpu.VMEM((wb // LANES, HALO_B + ts + SUBLANES, LANES), F32),
                        pltpu.VMEM((2, wb // LANES, (HALO_B + ts) // 2, LANES), jnp.uint32),
                        pltpu.VMEM((ts, wb), F32),
                        pltpu.VMEM((ts, 3 * wa + 2 * d), F32),
                        pltpu.VMEM((ts, wa), BF16),
                        pltpu.VMEM((ts, wb), BF16)],
        compiler_params=pltpu.CompilerParams(
            dimension_semantics=("arbitrary", "arbitrary"), vmem_limit_bytes=MIXER_VMEM_LIMIT),
        name="mixer",
    )(x, *consts)


def _zero_unwritten(used_ref, ys_ref, zero, sem):
    n_tiles = used_ref.shape[0]
    rb = zero.shape[0]

    def fill(row, n_rows):
        return pltpu.make_async_copy(zero.at[pl.ds(0, n_rows), :],
                                     ys_ref.at[pl.ds(row, n_rows), :], sem)

    dump = [fill(r, rb) for r in range(n_tiles * SORTED_ROWS, ys_ref.shape[0], rb)]

    def tails(wait):
        def body(i, carry):
            used = used_ref[i]
            tail = SORTED_ROWS - used
            row = i * SORTED_ROWS + used
            size = SORTED_ROWS - TOP_K * SEQ_TILE
            while size >= PIECE:
                @pl.when((tail & size) != 0)
                def _(row=row, size=size):
                    c = fill(pl.multiple_of(row, PIECE), size)
                    c.wait() if wait else c.start()
                row = row + (tail & size)
                size //= 2
            return carry
        lax.fori_loop(0, n_tiles, body, 0)

    zero[...] = jnp.zeros(zero.shape, F32)
    for c in dump:
        c.start()
    tails(False)
    for c in dump:
        c.wait()
    tails(True)


def _expert_kernel(bexp_ref, nused_ref, src_ref, dst_ref, used_ref, hs_ref, wg_ref, wu_ref, wd_ref,
                   ys_ref, xbuf, ybuf, wg_bf, wu_bf, wd_bf, sem_in, sem_out):
    b = pl.program_id(0)
    n_used = nused_ref[0]
    rb = xbuf.shape[1]

    @pl.when(b == 0)
    def _():
        _zero_unwritten(used_ref, ys_ref, ybuf.at[1], sem_out.at[1])

    def piece_copies(table_ref, blk, hbm_ref, buf, slot, sem, inbound):
        copies = []
        for j in range(PIECES_PER_BLOCK):
            row = pl.multiple_of(table_ref[blk * PIECES_PER_BLOCK + j] * PIECE, PIECE)
            hbm = hbm_ref.at[pl.ds(row, PIECE), :]
            vmem = buf.at[slot, pl.ds(j * PIECE, PIECE), :]
            copies.append(pltpu.make_async_copy(hbm, vmem, sem.at[slot]) if inbound
                          else pltpu.make_async_copy(vmem, hbm, sem.at[slot]))
        return copies

    def start_in(blk, slot):
        for c in piece_copies(src_ref, blk, hs_ref, xbuf, slot, sem_in, True):
            c.start()

    def start_out(blk, slot):
        for j, c in enumerate(piece_copies(dst_ref, blk, ys_ref, ybuf, slot, sem_out, False)):
            c.start(priority=j % 2)

    def wait_in(slot):
        pltpu.make_async_copy(hs_ref.at[pl.ds(0, rb), :], xbuf.at[slot], sem_in.at[slot]).wait()

    def wait_out(slot):
        pltpu.make_async_copy(ybuf.at[slot], ys_ref.at[pl.ds(0, rb), :], sem_out.at[slot]).wait()

    @pl.when(b < n_used)
    def _():
        slot = lax.rem(b, 2)
        other = 1 - slot
        prev = jnp.maximum(b - 1, 0)

        @pl.when(b == 0)
        def _():
            start_in(0, 0)

        @pl.when(b + 1 < n_used)
        def _():
            start_in(b + 1, other)

        @pl.when((b == 0) | (bexp_ref[prev] != bexp_ref[b]))
        def _():
            wg_bf[...] = wg_ref[...].astype(BF16)
            wu_bf[...] = wu_ref[...].astype(BF16)
            wd_bf[...] = wd_ref[...].astype(BF16)

        wait_in(slot)

        @pl.when(b >= 2)
        def _():
            wait_out(slot)

        xb = xbuf[slot].astype(BF16)
        a = _dot(xb, wg_bf[...])
        hid = (a * _sigmoid(a) * _dot(xb, wu_bf[...])).astype(BF16)
        ybuf[slot] = _dot(hid, wd_bf[...])
        start_out(b, slot)

        @pl.when(b == n_used - 1)
        def _():
            @pl.when(b >= 1)
            def _():
                wait_out(other)
            wait_out(slot)


def _experts(hs, w_gate, w_up, w_down, bexp, nused, src, dst, used_rows, n_out_rows):
    _, d = hs.shape
    _, _, de = w_gate.shape
    rb = EXPERT_TILE
    assert (n_out_rows - hs.shape[0]) % rb == 0
    wmap = lambda b, bexp, nused, src, dst, used: (bexp[b], 0, 0)
    grid_spec = pltpu.PrefetchScalarGridSpec(
        num_scalar_prefetch=5,
        grid=(bexp.shape[0],),
        in_specs=[pl.BlockSpec(memory_space=pl.ANY),
                  pl.BlockSpec((None, d, de), wmap),
                  pl.BlockSpec((None, d, de), wmap),
                  pl.BlockSpec((None, de, d), wmap)],
        out_specs=pl.BlockSpec(memory_space=pl.ANY),
        scratch_shapes=[pltpu.VMEM((2, rb, d), F32), pltpu.VMEM((2, rb, d), F32),
                        pltpu.VMEM((d, de), BF16), pltpu.VMEM((d, de), BF16),
                        pltpu.VMEM((de, d), BF16),
                        pltpu.SemaphoreType.DMA((2,)), pltpu.SemaphoreType.DMA((2,))],
    )
    return pl.pallas_call(
        _expert_kernel,
        grid_spec=grid_spec,
        out_shape=jax.ShapeDtypeStruct((n_out_rows, d), F32),
        compiler_params=pltpu.CompilerParams(dimension_semantics=("arbitrary",),
                                             vmem_limit_bytes=VMEM_LIMIT),
        name="experts",
    )(bexp, nused, src, dst, used_rows, hs, w_gate, w_up, w_down)


def _combine_kernel(x1_ref, p_ref, lrow_ref, gw_ref, g3_ref, wpg_ref, wpp_ref, gf_ref, ys_ref,
                    out_ref):
    ts = x1_ref.shape[0]
    riota = lax.broadcasted_iota(jnp.int32, (ys_ref.shape[0], ts), 0)
    pick = jnp.where(riota == lrow_ref[0:1, :], gw_ref[0:1, :],
                     jnp.where(riota == lrow_ref[1:2, :], gw_ref[1:2, :], 0.0))
    moe = lax.dot_general(pick.astype(BF16), ys_ref[...].astype(BF16), (((0,), (0,)), ((), ())),
                          preferred_element_type=F32)
    x2 = x1_ref[...] + moe
    hp = _rms(x2, g3_ref[...]).astype(BF16)
    proj = _dot(p_ref[...].astype(BF16), wpp_ref[...])
    x3 = x2 + _sigmoid(_dot(hp, wpg_ref[...])) * proj
    out_ref[...] = _rms(x3, gf_ref[...])


def _combine(x1, p, lrow, gw, ys, g3, w_pg, w_pp, gf):
    t, d = x1.shape
    pd = p.shape[1]
    ts = SEQ_TILE
    tile = lambda i: (i, 0)
    const = lambda i: (0, 0)
    return pl.pallas_call(
        _combine_kernel,
        grid=(t // ts,),
        in_specs=[pl.BlockSpec((ts, d), tile),
                  pl.BlockSpec((ts, pd), tile),
                  pl.BlockSpec((TOP_K, ts), lambda i: (0, i)),
                  pl.BlockSpec((TOP_K, ts), lambda i: (0, i)),
                  pl.BlockSpec((1, d), const),
                  pl.BlockSpec((d, d), const),
                  pl.BlockSpec((pd, d), const),
                  pl.BlockSpec((1, d), const),
                  pl.BlockSpec((SORTED_ROWS, d), tile)],
        out_specs=pl.BlockSpec((ts, d), tile),
        out_shape=jax.ShapeDtypeStruct((t, d), F32),
        compiler_params=pltpu.CompilerParams(dimension_semantics=("arbitrary",),
                                             vmem_limit_bytes=VMEM_LIMIT),
        name="combine",
    )(x1, p, lrow, gw, g3, w_pg, w_pp, gf, ys)


def _piece_plan(n_pieces):
    n_tiles = n_pieces.shape[0]
    ppb = PIECES_PER_BLOCK
    tile_pieces = SORTED_ROWS // PIECE
    i32 = jnp.int32
    excl = lambda v, axis: jnp.cumsum(v, axis=axis, dtype=i32) - v
    base = (jnp.arange(n_tiles, dtype=i32) * tile_pieces)[:, None] + excl(n_pieces, 1)
    per_expert = n_pieces.T
    blocks = (jnp.sum(per_expert, axis=1, dtype=i32) + ppb - 1) // ppb
    first_block = excl(blocks, 0)
    n_used = jnp.sum(blocks, dtype=i32)

    max_blocks = (TOP_K * n_tiles * SEQ_TILE // PIECE + n_tiles * N_EXPERTS) // ppb + N_EXPERTS
    block_ids = jnp.arange(max_blocks, dtype=i32)
    blk = jnp.minimum(block_ids, n_used - 1)
    bexp = jnp.sum((first_block[None, :] <= blk[:, None]).astype(i32), axis=1) - 1
    strip_lo = jnp.take(excl(per_expert, 1), bexp, axis=0)[:, :, None]
    strip_len = jnp.take(per_expert, bexp, axis=0)[:, :, None]
    strip_base = jnp.take(base.T, bexp, axis=0)[:, :, None]
    local = ((block_ids - jnp.take(first_block, bexp)) * ppb)[:, None] + jnp.arange(ppb, dtype=i32)
    local = local[:, None, :]
    inside = (local >= strip_lo) & (local < strip_lo + strip_len)
    src = jnp.sum(jnp.where(inside, strip_base + local - strip_lo, 0), axis=1, dtype=i32).reshape(-1)
    real = jnp.any(inside, axis=1).reshape(-1)
    n_dump = 2 * ppb
    dump = (block_ids % 2)[:, None] * ppb + jnp.arange(ppb, dtype=i32)[None, :]
    dst = jnp.where(real, src, n_tiles * tile_pieces + dump.reshape(-1))
    n_out_rows = (n_tiles * tile_pieces + n_dump) * PIECE
    return bexp, n_used.reshape(1), src, dst, n_out_rows


def _layer(x, p, mix_norm_g, w_in, conv_a_w, w_out_a, conv_b_w, conv_b_b, ln_b_g, ln_b_b, w_out_b,
           b_gate, w_o, ffn_norm_g, w_rg, b_rg, w_re, b_re, w_eg, w_eu, w_ed, ple_norm_g, w_pg, w_pp,
           out_norm_g):
    bsz, seq, d = x.shape
    t = bsz * seq
    row = lambda v: v.reshape(1, -1).astype(F32)
    w_r = jnp.zeros((ROUTER_ROWS, d), F32).at[0:N_GROUPS].set(w_rg.T).at[8:].set(w_re.T)
    b_r = jnp.zeros((ROUTER_ROWS,), F32).at[0:N_GROUPS].set(b_rg).at[8:].set(b_re)
    b_r = jnp.broadcast_to(b_r[:, None], (ROUTER_ROWS, LANES))

    x1, lrow, gw, n_pieces, hs = _mixer(
        x, row(mix_norm_g), w_in.astype(BF16), conv_a_w, w_out_a.astype(BF16), conv_b_w,
        row(conv_b_b), row(ln_b_g), row(ln_b_b), w_out_b.astype(BF16), row(b_gate),
        w_o.astype(BF16), row(ffn_norm_g), w_r, b_r)
    x1 = x1.reshape(t, d)

    n_pieces = n_pieces[:, :, 0]
    bexp, n_used, src, dst, n_out_rows = _piece_plan(n_pieces)
    used_rows = jnp.sum(n_pieces, axis=1, dtype=jnp.int32) * PIECE
    ys = _experts(hs, w_eg, w_eu, w_ed, bexp, n_used, src, dst, used_rows, n_out_rows)
    out = _combine(x1, p.reshape(t, -1), lrow, gw, ys, row(ple_norm_g),
                   w_pg.astype(BF16), w_pp.astype(BF16), row(out_norm_g))
    return out.reshape(bsz, seq, d)


def kernel(x, p, mix_norm_g, w_in, conv_a_w, w_out_a, conv_b_w, conv_b_b, ln_b_g, ln_b_b, w_out_b, b_gate, w_o, ffn_norm_g, w_router_group, b_router_group, w_router_expert, b_router_expert, w_exp_gate, w_exp_up, w_exp_down, ple_norm_g, w_ple_gate, w_ple_proj, final_norm_g):
    assert w_in.shape[0] == 1, "the combine kernel fuses the final RMSNorm: single layer only"
    return _layer(x, p[0], mix_norm_g[0], w_in[0], conv_a_w[0], w_out_a[0], conv_b_w[0],
                  conv_b_b[0], ln_b_g[0], ln_b_b[0], w_out_b[0], b_gate[0], w_o[0], ffn_norm_g[0],
                  w_router_group[0], b_router_group[0], w_router_expert[0], b_router_expert[0],
                  w_exp_gate[0], w_exp_up[0], w_exp_down[0], ple_norm_g[0], w_ple_gate[0],
                  w_ple_proj[0], final_norm_g)
```

```python
import functools

import jax
import jax.numpy as jnp
from jax import lax
from jax.experimental import pallas as pl
from jax.experimental.pallas import tpu as pltpu

F32 = jnp.float32
BF16 = jnp.bfloat16
EPS = 1e-6

LANES = 128
SUBLANES = 8
N_GROUPS = 4
EXPERTS_PER_GROUP = 8
N_EXPERTS = N_GROUPS * EXPERTS_PER_GROUP
TOP_K = 2
ROUTER_ROWS = 8 + N_EXPERTS

SEQ_TILE = 512
CONV_ROWS = 128
MXU_COLS = 256
RAW_SPLIT = (0, 1, 2, 5, 12)
HALO_A = 8
HALO_B = 32
TAIL_ROWS = 128
PIECE = SUBLANES
EXPERT_TILE = 512
PIECES_PER_BLOCK = EXPERT_TILE // PIECE
SORTED_ROWS = TOP_K * SEQ_TILE + N_EXPERTS * PIECE
VMEM_LIMIT = 56 * 1024 * 1024
MIXER_VMEM_LIMIT = 60 * 1024 * 1024


def _sigmoid(v):
    return 1.0 / (1.0 + jnp.exp(-v))


def _rms(v, g):
    return v * lax.rsqrt(jnp.mean(v * v, axis=-1, keepdims=True) + EPS) * g


def _dot(a, b):
    return jnp.dot(a, b, preferred_element_type=F32)


def _conv_steps(buf, w_ref, width, halo, rows, emit):
    steps = []
    for j in range(buf.shape[0]):
        for base in range(0, rows, CONV_ROWS):
            def step(j=j, base=base):
                lanes = slice(j * LANES, (j + 1) * LANES)
                acc = None
                for k in range(width):
                    start = base + halo - (width - 1) + k
                    term = w_ref[k:k + 1, lanes] * buf[j, start:start + CONV_ROWS, :]
                    acc = term if acc is None else acc + term
                emit(slice(base, base + CONV_ROWS), lanes, acc)
            steps.append((width * CONV_ROWS // 8, step))
    return steps


def _packed_conv_steps(packed, w_ref, width, halo, rows, emit):
    group = 2 * SUBLANES
    steps = []
    for j in range(packed.shape[1]):
        for base in range(0, rows, CONV_ROWS):
            def step(j=j, base=base):
                accs = [None] * (CONV_ROWS // group)
                for k in range(width):
                    shift = k % 2
                    tap = w_ref[k, j]
                    for g in range(len(accs)):
                        word = (base + g * group + halo - (width - 1) + k - shift) // 2
                        vals = pltpu.bitcast(packed[shift, j, word:word + SUBLANES, :], BF16)
                        accs[g] = vals * tap if accs[g] is None else accs[g] + vals * tap
                for g, acc in enumerate(accs):
                    lo = base + g * group
                    emit(slice(lo, lo + group), slice(j * LANES, (j + 1) * LANES), acc.astype(F32))
            steps.append((width * CONV_ROWS // 16, step))
    return steps


def _interleave(mxu_steps, vpu_steps):
    total_m = sum(c for c, _ in mxu_steps)
    total_v = sum(c for c, _ in vpu_steps)
    done_m = 0
    done_v = 0
    pending = list(vpu_steps)
    for cost, step in mxu_steps:
        step()
        done_m += cost
        while pending and done_v * total_m < done_m * total_v:
            c, vstep = pending.pop(0)
            vstep()
            done_v += c
    for _, vstep in pending:
        vstep()


def _mixer_kernel(x_ref, g1_ref, win_ref, caw_ref, woa_ref, cbw_ref, cbb_ref, lng_ref, lnb_ref,
                  wob_ref, bg_ref, wo_ref, g2_ref, wr_ref, br_ref, before_ref,
                  x1_ref, lrow_ref, gw_ref, np_ref, hs_ref,
                  abuf, bbuf, bpk, cbuf, pbuf, zbuf, swbuf, *, width_a, width_b, d_model, wa, wb):
    ts = x_ref.shape[0]

    @pl.when(pl.program_id(1) == 0)
    def _():
        abuf[:, 0:HALO_A, :] = jnp.zeros((abuf.shape[0], HALO_A, LANES), F32)
        bbuf[:, 0:HALO_B, :] = jnp.zeros((bbuf.shape[0], HALO_B, LANES), F32)
        bbuf[:, HALO_B + ts:, :] = jnp.zeros((bbuf.shape[0], SUBLANES, LANES), F32)

    h = _rms(x_ref[...], g1_ref[...]).astype(BF16)
    col_b, col_c, col_x = 0, wa, 2 * wa
    col_v, col_g, col_gate = 3 * wa, 3 * wa + wb, 3 * wa + 2 * wb

    def proj(col, q):
        return _dot(h, win_ref[:, col + q * MXU_COLS:col + (q + 1) * MXU_COLS])

    raw_cols = ((col_c, wa), (col_x, wa), (col_b, wa), (col_gate, 2 * d_model))
    pb_c, pb_x, pb_b, pb_g = 0, wa, 2 * wa, 3 * wa
    raw_steps = []
    off = 0
    for col, width in raw_cols:
        for q in range(width // MXU_COLS):
            def raw_step(col=col, q=q, off=off):
                lo = off + q * MXU_COLS
                pbuf[:, lo:lo + MXU_COLS] = proj(col, q)
            raw_steps.append((1, raw_step))
        off += width

    def emit_b(rows, lanes, acc):
        cbuf[rows, lanes] = acc

    conv_steps = _packed_conv_steps(bpk, cbw_ref, width_b, HALO_B, ts, emit_b)

    n_glu = wb // MXU_COLS
    lanes_per_glu = MXU_COLS // LANES
    assert len(RAW_SPLIT) == n_glu + 1 and sum(RAW_SPLIT) == len(raw_steps)
    raw_at = [sum(RAW_SPLIT[:k]) for k in range(len(RAW_SPLIT) + 1)]
    for _, raw in raw_steps[:raw_at[1]]:
        raw()
    for q in range(n_glu):
        u = proj(col_v, q) * _sigmoid(proj(col_g, q))
        for jj in range(lanes_per_glu):
            j = q * lanes_per_glu + jj
            bbuf[j, HALO_B:HALO_B + ts, :] = u[:, jj * LANES:(jj + 1) * LANES]
            for shift in range(2):
                rows_f32 = bbuf[j, shift:shift + HALO_B + ts, :]
                bpk[shift, j] = pltpu.bitcast(rows_f32.astype(BF16), jnp.uint32)
            bbuf[j, 0:HALO_B, :] = bbuf[j, ts:ts + HALO_B, :]
        _interleave(raw_steps[raw_at[q + 1]:raw_at[q + 2]],
                    conv_steps[q * len(conv_steps) // n_glu:(q + 1) * len(conv_steps) // n_glu])

    for j in range(wa // LANES):
        abuf[j, HALO_A:HALO_A + ts, :] = (pbuf[:, pb_c + j * LANES:pb_c + (j + 1) * LANES]
                                           * pbuf[:, pb_x + j * LANES:pb_x + (j + 1) * LANES])

    def emit_a(rows, lanes, acc):
        zbuf[rows, lanes] = (pbuf[rows, pb_b + lanes.start:pb_b + lanes.stop] * acc).astype(BF16)

    for _, step in _conv_steps(abuf, caw_ref, width_a, HALO_A, ts, emit_a):
        step()
    abuf[:, 0:HALO_A, :] = abuf[:, ts:ts + HALO_A, :]

    for r in range(0, ts, TAIL_ROWS):
        rows = slice(r, r + TAIL_ROWS)
        c = cbuf[rows, :] + cbb_ref[...]
        mu = jnp.mean(c, axis=-1, keepdims=True)
        cc = c - mu
        var = jnp.mean(cc * cc, axis=-1, keepdims=True)
        ln = cc * lax.rsqrt(var + EPS) * lng_ref[...] + lnb_ref[...]
        swbuf[rows, :] = (ln * _sigmoid(ln)).astype(BF16)
        gcols = slice(pb_g, pb_g + 2 * d_model)
        pbuf[rows, gcols] = _sigmoid(pbuf[rows, gcols] + bg_ref[...])
    mix = (pbuf[:, pb_g:pb_g + d_model] * _dot(zbuf[...], woa_ref[...])
           + pbuf[:, pb_g + d_model:pb_g + 2 * d_model] * _dot(swbuf[...], wob_ref[...]))

    x1 = x_ref[...] + _dot(mix.astype(BF16), wo_ref[...])
    x1_ref[...] = x1

    h2 = _rms(x1, g2_ref[...])
    logits = lax.dot_general(wr_ref[...], h2, (((1,), (1,)), ((), ())),
                             preferred_element_type=F32) + br_ref[:, 0:1]
    gl = logits[0:N_GROUPS, :]
    gmax = jnp.max(gl, axis=0, keepdims=True)
    giota = lax.broadcasted_iota(jnp.int32, gl.shape, 0)
    grp = jnp.min(jnp.where(gl == gmax, giota, N_GROUPS), axis=0, keepdims=True)
    p_grp = 1.0 / jnp.sum(jnp.exp(gl - gmax), axis=0, keepdims=True)

    el = logits[8:8 + N_EXPERTS, :]
    eiota = lax.broadcasted_iota(jnp.int32, el.shape, 0)
    neg = jnp.float32(-jnp.inf)
    m0 = jnp.where(lax.div(eiota, EXPERTS_PER_GROUP) == grp, el, neg)
    v1 = jnp.max(m0, axis=0, keepdims=True)
    i1 = jnp.min(jnp.where(m0 == v1, eiota, N_EXPERTS), axis=0, keepdims=True)
    m1 = jnp.where(eiota == i1, neg, m0)
    v2 = jnp.max(m1, axis=0, keepdims=True)
    i2 = jnp.min(jnp.where(m1 == v2, eiota, N_EXPERTS), axis=0, keepdims=True)
    e2 = jnp.exp(v2 - v1)
    w1 = p_grp / (1.0 + e2)
    w2 = p_grp * e2 / (1.0 + e2)

    oh1 = eiota == i1
    oh2 = eiota == i2
    used = jnp.where(oh1, 1.0, jnp.where(oh2, 1.0, 0.0))
    in_strip = _dot(used.astype(BF16), before_ref[...])
    pieces = jnp.floor((jnp.sum(used, axis=1, keepdims=True) + (PIECE - 1.0)) * (1.0 / PIECE))
    er = lax.broadcasted_iota(jnp.int32, (N_EXPERTS, N_EXPERTS), 0)
    ec = lax.broadcasted_iota(jnp.int32, (N_EXPERTS, N_EXPERTS), 1)
    earlier = jnp.where(ec < er, 1.0, 0.0)
    strip_start = _dot(earlier, jnp.broadcast_to(pieces * PIECE, (N_EXPERTS, LANES)))[:, 0:1]
    row = strip_start + in_strip
    r1 = jnp.sum(jnp.where(oh1, row, 0.0), axis=0, keepdims=True)
    r2 = jnp.sum(jnp.where(oh2, row, 0.0), axis=0, keepdims=True)

    lrow = jnp.concatenate([r1, r2], axis=0).astype(jnp.int32)
    lrow_ref[...] = lrow
    gw_ref[...] = jnp.concatenate([w1, w2], axis=0)
    np_ref[...] = jnp.broadcast_to(pieces, np_ref.shape).astype(jnp.int32)

    riota = lax.broadcasted_iota(jnp.int32, (hs_ref.shape[0], ts), 0)
    pick = jnp.where(riota == lrow[0:1, :], 1.0, jnp.where(riota == lrow[1:2, :], 1.0, 0.0))
    hs_ref[...] = _dot(pick.astype(BF16), h2.astype(BF16))


def _const_spec(shape):
    return pl.BlockSpec(shape, lambda *_: (0,) * len(shape), pipeline_mode=pl.Buffered(1))


def _mixer(x, g1, w_in, conv_a_w, w_out_a, conv_b_w, conv_b_b, ln_g, ln_b, w_out_b, b_gate, w_o,
           g2, w_r, b_r):
    bsz, seq, d = x.shape
    wa = w_out_a.shape[0]
    wb = w_out_b.shape[0]
    ts = SEQ_TILE
    n_s = seq // ts
    t = bsz * seq
    taps_b = conv_b_w.astype(BF16).reshape(conv_b_w.shape[0], wb // LANES, 1, LANES)
    taps_b = jnp.broadcast_to(taps_b, taps_b.shape[:2] + (2 * SUBLANES, LANES))
    before = jnp.triu(jnp.ones((ts, ts), BF16), k=1)
    consts = (g1, w_in, conv_a_w, w_out_a, taps_b, conv_b_b, ln_g, ln_b, w_out_b, b_gate, w_o,
              g2, w_r, b_r, before)
    tok_spec = pl.BlockSpec((TOP_K, ts), lambda b, s: (0, b * n_s + s))
    return pl.pallas_call(
        functools.partial(_mixer_kernel, width_a=conv_a_w.shape[0], width_b=conv_b_w.shape[0],
                          d_model=d, wa=wa, wb=wb),
        grid=(bsz, n_s),
        in_specs=[pl.BlockSpec((None, ts, d), lambda b, s: (b, s, 0))]
                 + [_const_spec(c.shape) for c in consts],
        out_specs=[pl.BlockSpec((None, ts, d), lambda b, s: (b, s, 0)),
                   tok_spec, tok_spec,
                   pl.BlockSpec((None, N_EXPERTS, LANES), lambda b, s: (b * n_s + s, 0, 0)),
                   pl.BlockSpec((SORTED_ROWS, d), lambda b, s: (b * n_s + s, 0))],
        out_shape=[jax.ShapeDtypeStruct((bsz, seq, d), F32),
                   jax.ShapeDtypeStruct((TOP_K, t), jnp.int32),
                   jax.ShapeDtypeStruct((TOP_K, t), F32),
                   jax.ShapeDtypeStruct((t // ts, N_EXPERTS, LANES), jnp.int32),
                   jax.ShapeDtypeStruct((t // ts * SORTED_ROWS, d), F32)],
        scratch_shapes=[pltpu.VMEM((wa // LANES, HALO_A + ts, LANES), F32),
                        pltpu.VMEM((wb // LANES, HALO_B + ts + SUBLANES, LANES), F32),
                        pltpu.VMEM((2, wb // LANES, (HALO_B + ts) // 2, LANES), jnp.uint32),
                        pltpu.VMEM((ts, wb), F32),
                        pltpu.VMEM((ts, 3 * wa + 2 * d), F32),
                        pltpu.VMEM((ts, wa), BF16),
                        pltpu.VMEM((ts, wb), BF16)],
        compiler_params=pltpu.CompilerParams(
            dimension_semantics=("arbitrary", "arbitrary"), vmem_limit_bytes=MIXER_VMEM_LIMIT),
        name="mixer",
    )(x, *consts)


def _zero_unwritten(used_ref, ys_ref, zero, sem):
    n_tiles = used_ref.shape[0]
    rb = zero.shape[0]

    def fill(row, n_rows):
        return pltpu.make_async_copy(zero.at[pl.ds(0, n_rows), :],
                                     ys_ref.at[pl.ds(row, n_rows), :], sem)

    dump = [fill(r, rb) for r in range(n_tiles * SORTED_ROWS, ys_ref.shape[0], rb)]

    def tails(wait):
        def body(i, carry):
            used = used_ref[i]
            tail = SORTED_ROWS - used
            row = i * SORTED_ROWS + used
            size = SORTED_ROWS - TOP_K * SEQ_TILE
            while size >= PIECE:
                @pl.when((tail & size) != 0)
                def _(row=row, size=size):
                    c = fill(pl.multiple_of(row, PIECE), size)
                    c.wait() if wait else c.start()
                row = row + (tail & size)
                size //= 2
            return carry
        lax.fori_loop(0, n_tiles, body, 0)

    zero[...] = jnp.zeros(zero.shape, F32)
    for c in dump:
        c.start()
    tails(False)
    for c in dump:
        c.wait()
    tails(True)


def _expert_kernel(bexp_ref, nused_ref, src_ref, dst_ref, used_ref, hs_ref, wg_ref, wu_ref, wd_ref,
                   ys_ref, xbuf, ybuf, wg_bf, wu_bf, wd_bf, sem_in, sem_out):
    b = pl.program_id(0)
    n_used = nused_ref[0]
    rb = xbuf.shape[1]

    @pl.when(b == 0)
    def _():
        _zero_unwritten(used_ref, ys_ref, ybuf.at[1], sem_out.at[1])

    def piece_copies(table_ref, blk, hbm_ref, buf, slot, sem, inbound):
        copies = []
        for j in range(PIECES_PER_BLOCK):
            row = pl.multiple_of(table_ref[blk * PIECES_PER_BLOCK + j] * PIECE, PIECE)
            hbm = hbm_ref.at[pl.ds(row, PIECE), :]
            vmem = buf.at[slot, pl.ds(j * PIECE, PIECE), :]
            copies.append(pltpu.make_async_copy(hbm, vmem, sem.at[slot]) if inbound
                          else pltpu.make_async_copy(vmem, hbm, sem.at[slot]))
        return copies

    def start_in(blk, slot):
        for c in piece_copies(src_ref, blk, hs_ref, xbuf, slot, sem_in, True):
            c.start(priority=1)

    def start_out(blk, slot):
        for c in piece_copies(dst_ref, blk, ys_ref, ybuf, slot, sem_out, False):
            c.start()

    def wait_in(slot):
        pltpu.make_async_copy(hs_ref.at[pl.ds(0, rb), :], xbuf.at[slot], sem_in.at[slot]).wait()

    def wait_out(slot):
        pltpu.make_async_copy(ybuf.at[slot], ys_ref.at[pl.ds(0, rb), :], sem_out.at[slot]).wait()

    @pl.when(b < n_used)
    def _():
        slot = lax.rem(b, 2)
        other = 1 - slot
        prev = jnp.maximum(b - 1, 0)

        @pl.when(b == 0)
        def _():
            start_in(0, 0)

        @pl.when(b + 1 < n_used)
        def _():
            start_in(b + 1, other)

        @pl.when((b == 0) | (bexp_ref[prev] != bexp_ref[b]))
        def _():
            wg_bf[...] = wg_ref[...].astype(BF16)
            wu_bf[...] = wu_ref[...].astype(BF16)
            wd_bf[...] = wd_ref[...].astype(BF16)

        wait_in(slot)

        @pl.when(b >= 2)
        def _():
            wait_out(slot)

        xb = xbuf[slot].astype(BF16)
        a = _dot(xb, wg_bf[...])
        hid = (a * _sigmoid(a) * _dot(xb, wu_bf[...])).astype(BF16)
        ybuf[slot] = _dot(hid, wd_bf[...])
        start_out(b, slot)

        @pl.when(b == n_used - 1)
        def _():
            @pl.when(b >= 1)
            def _():
                wait_out(other)
            wait_out(slot)


def _experts(hs, w_gate, w_up, w_down, bexp, nused, src, dst, used_rows, n_out_rows):
    _, d = hs.shape
    _, _, de = w_gate.shape
    rb = EXPERT_TILE
    assert (n_out_rows - hs.shape[0]) % rb == 0
    wmap = lambda b, bexp, nused, src, dst, used: (bexp[b], 0, 0)
    grid_spec = pltpu.PrefetchScalarGridSpec(
        num_scalar_prefetch=5,
        grid=(bexp.shape[0],),
        in_specs=[pl.BlockSpec(memory_space=pl.ANY),
                  pl.BlockSpec((None, d, de), wmap),
                  pl.BlockSpec((None, d, de), wmap),
                  pl.BlockSpec((None, de, d), wmap)],
        out_specs=pl.BlockSpec(memory_space=pl.ANY),
        scratch_shapes=[pltpu.VMEM((2, rb, d), F32), pltpu.VMEM((2, rb, d), F32),
                        pltpu.VMEM((d, de), BF16), pltpu.VMEM((d, de), BF16),
                        pltpu.VMEM((de, d), BF16),
                        pltpu.SemaphoreType.DMA((2,)), pltpu.SemaphoreType.DMA((2,))],
    )
    return pl.pallas_call(
        _expert_kernel,
        grid_spec=grid_spec,
        out_shape=jax.ShapeDtypeStruct((n_out_rows, d), F32),
        compiler_params=pltpu.CompilerParams(dimension_semantics=("arbitrary",),
                                             vmem_limit_bytes=VMEM_LIMIT),
        name="experts",
    )(bexp, nused, src, dst, used_rows, hs, w_gate, w_up, w_down)


def _combine_kernel(x1_ref, p_ref, lrow_ref, gw_ref, g3_ref, wpg_ref, wpp_ref, gf_ref, ys_ref,
                    out_ref):
    ts = x1_ref.shape[0]
    riota = lax.broadcasted_iota(jnp.int32, (ys_ref.shape[0], ts), 0)
    pick = jnp.where(riota == lrow_ref[0:1, :], gw_ref[0:1, :],
                     jnp.where(riota == lrow_ref[1:2, :], gw_ref[1:2, :], 0.0))
    moe = lax.dot_general(pick.astype(BF16), ys_ref[...].astype(BF16), (((0,), (0,)), ((), ())),
                          preferred_element_type=F32)
    x2 = x1_ref[...] + moe
    hp = _rms(x2, g3_ref[...]).astype(BF16)
    proj = _dot(p_ref[...].astype(BF16), wpp_ref[...])
    x3 = x2 + _sigmoid(_dot(hp, wpg_ref[...])) * proj
    out_ref[...] = _rms(x3, gf_ref[...])


def _combine(x1, p, lrow, gw, ys, g3, w_pg, w_pp, gf):
    t, d = x1.shape
    pd = p.shape[1]
    ts = SEQ_TILE
    tile = lambda i: (i, 0)
    const = lambda i: (0, 0)
    return pl.pallas_call(
        _combine_kernel,
        grid=(t // ts,),
        in_specs=[pl.BlockSpec((ts, d), tile),
                  pl.BlockSpec((ts, pd), tile),
                  pl.BlockSpec((TOP_K, ts), lambda i: (0, i)),
                  pl.BlockSpec((TOP_K, ts), lambda i: (0, i)),
                  pl.BlockSpec((1, d), const),
                  pl.BlockSpec((d, d), const),
                  pl.BlockSpec((pd, d), const),
                  pl.BlockSpec((1, d), const),
                  pl.BlockSpec((SORTED_ROWS, d), tile)],
        out_specs=pl.BlockSpec((ts, d), tile),
        out_shape=jax.ShapeDtypeStruct((t, d), F32),
        compiler_params=pltpu.CompilerParams(dimension_semantics=("arbitrary",),
                                             vmem_limit_bytes=VMEM_LIMIT),
        name="combine",
    )(x1, p, lrow, gw, g3, w_pg, w_pp, gf, ys)


def _piece_plan(n_pieces):
    n_tiles = n_pieces.shape[0]
    ppb = PIECES_PER_BLOCK
    tile_pieces = SORTED_ROWS // PIECE
    i32 = jnp.int32
    excl = lambda v, axis: jnp.cumsum(v, axis=axis, dtype=i32) - v
    base = (jnp.arange(n_tiles, dtype=i32) * tile_pieces)[:, None] + excl(n_pieces, 1)
    per_expert = n_pieces.T
    blocks = (jnp.sum(per_expert, axis=1, dtype=i32) + ppb - 1) // ppb
    first_block = excl(blocks, 0)
    n_used = jnp.sum(blocks, dtype=i32)

    max_blocks = (TOP_K * n_tiles * SEQ_TILE // PIECE + n_tiles * N_EXPERTS) // ppb + N_EXPERTS
    block_ids = jnp.arange(max_blocks, dtype=i32)
    blk = jnp.minimum(block_ids, n_used - 1)
    bexp = jnp.sum((first_block[None, :] <= blk[:, None]).astype(i32), axis=1) - 1
    strip_lo = jnp.take(excl(per_expert, 1), bexp, axis=0)[:, :, None]
    strip_len = jnp.take(per_expert, bexp, axis=0)[:, :, None]
    strip_base = jnp.take(base.T, bexp, axis=0)[:, :, None]
    local = ((block_ids - jnp.take(first_block, bexp)) * ppb)[:, None] + jnp.arange(ppb, dtype=i32)
    local = local[:, None, :]
    inside = (local >= strip_lo) & (local < strip_lo + strip_len)
    src = jnp.sum(jnp.where(inside, strip_base + local - strip_lo, 0), axis=1, dtype=i32).reshape(-1)
    real = jnp.any(inside, axis=1).reshape(-1)
    n_dump = 2 * ppb
    dump = (block_ids % 2)[:, None] * ppb + jnp.arange(ppb, dtype=i32)[None, :]
    dst = jnp.where(real, src, n_tiles * tile_pieces + dump.reshape(-1))
    n_out_rows = (n_tiles * tile_pieces + n_dump) * PIECE
    return bexp, n_used.reshape(1), src, dst, n_out_rows


def _layer(x, p, mix_norm_g, w_in, conv_a_w, w_out_a, conv_b_w, conv_b_b, ln_b_g, ln_b_b, w_out_b,
           b_gate, w_o, ffn_norm_g, w_rg, b_rg, w_re, b_re, w_eg, w_eu, w_ed, ple_norm_g, w_pg, w_pp,
           out_norm_g):
    bsz, seq, d = x.shape
    t = bsz * seq
    row = lambda v: v.reshape(1, -1).astype(F32)
    w_r = jnp.zeros((ROUTER_ROWS, d), F32).at[0:N_GROUPS].set(w_rg.T).at[8:].set(w_re.T)
    b_r = jnp.zeros((ROUTER_ROWS,), F32).at[0:N_GROUPS].set(b_rg).at[8:].set(b_re)
    b_r = jnp.broadcast_to(b_r[:, None], (ROUTER_ROWS, LANES))

    x1, lrow, gw, n_pieces, hs = _mixer(
        x, row(mix_norm_g), w_in.astype(BF16), conv_a_w, w_out_a.astype(BF16), conv_b_w,
        row(conv_b_b), row(ln_b_g), row(ln_b_b), w_out_b.astype(BF16), row(b_gate),
        w_o.astype(BF16), row(ffn_norm_g), w_r, b_r)
    x1 = x1.reshape(t, d)

    n_pieces = n_pieces[:, :, 0]
    bexp, n_used, src, dst, n_out_rows = _piece_plan(n_pieces)
    used_rows = jnp.sum(n_pieces, axis=1, dtype=jnp.int32) * PIECE
    ys = _experts(hs, w_eg, w_eu, w_ed, bexp, n_used, src, dst, used_rows, n_out_rows)
    out = _combine(x1, p.reshape(t, -1), lrow, gw, ys, row(ple_norm_g),
                   w_pg.astype(BF16), w_pp.astype(BF16), row(out_norm_g))
    return out.reshape(bsz, seq, d)


def kernel(x, p, mix_norm_g, w_in, conv_a_w, w_out_a, conv_b_w, conv_b_b, ln_b_g, ln_b_b, w_out_b, b_gate, w_o, ffn_norm_g, w_router_group, b_router_group, w_router_expert, b_router_expert, w_exp_gate, w_exp_up, w_exp_down, ple_norm_g, w_ple_gate, w_ple_proj, final_norm_g):
    assert w_in.shape[0] == 1, "the combine kernel fuses the final RMSNorm: single layer only"
    return _layer(x, p[0], mix_norm_g[0], w_in[0], conv_a_w[0], w_out_a[0], conv_b_w[0],
                  conv_b_b[0], ln_b_g[0], ln_b_b[0], w_out_b[0], b_gate[0], w_o[0], ffn_norm_g[0],
                  w_router_group[0], b_router_group[0], w_router_expert[0], b_router_expert[0],
                  w_exp_gate[0], w_exp_up[0], w_exp_down[0], ple_norm_g[0], w_ple_gate[0],
                  w_ple_proj[0], final_norm_g)
```
